```python
import jax, jax.numpy as jnp
from jax import lax
import numpy as np

D_MODEL = 1024
BATCH = 4
SEQ = 8192
DEPTH = 1

MIX_WIDTH = D_MODEL
HEAD_DIM = 64
N_HEADS = (MIX_WIDTH // 2) // HEAD_DIM
N_KV_HEADS = 2
ATTN_WIDTH = N_HEADS * HEAD_DIM
KV_WIDTH = N_KV_HEADS * HEAD_DIM
CONV_CH = MIX_WIDTH - ATTN_WIDTH
CONV_GROUPS = CONV_CH // HEAD_DIM
CONV_W = 31
IN_WIDTH = ATTN_WIDTH + 2 * KV_WIDTH + 2 * CONV_CH
Q_BLOCK = 128
GRID_W = 64
ROPE_THETA = 10000.0
ROPE_AXIS_DIM = HEAD_DIM // 2
N_KEYS = 128
N_EXPERTS = N_KEYS * N_KEYS
PEER_HEADS = 8
PEER_DQ = 256
PEER_TOPK = 16
TOKEN_CHUNK = 128
EPS = 1e-6

kernel_name = "hybrid_conv_gqa_peer_encoder"


def _rmsnorm(x, g):
    xf = x.astype(jnp.float32)
    y = xf * lax.rsqrt(jnp.mean(xf * xf, axis=-1, keepdims=True) + EPS)
    return (y * g.astype(jnp.float32)).astype(x.dtype)


def _layernorm(x, g, b):
    xf = x.astype(jnp.float32)
    mu = jnp.mean(xf, axis=-1, keepdims=True)
    var = jnp.mean(jnp.square(xf - mu), axis=-1, keepdims=True)
    y = (xf - mu) * lax.rsqrt(var + EPS)
    return (y * g.astype(jnp.float32) + b.astype(jnp.float32)).astype(x.dtype)


def _rope_rotate(x, ang):
    p = ang.shape[-1]
    cos = jnp.cos(ang)[:, None, :]
    sin = jnp.sin(ang)[:, None, :]
    xf = x.astype(jnp.float32)
    x1, x2 = xf[..., :p], xf[..., p:]
    return jnp.concatenate([x1 * cos - x2 * sin, x2 * cos + x1 * sin], axis=-1)


def _axial_rope(x, ang_row, ang_col):
    r = _rope_rotate(x[..., :ROPE_AXIS_DIM], ang_row)
    c = _rope_rotate(x[..., ROPE_AXIS_DIM:], ang_col)
    return jnp.concatenate([r, c], axis=-1).astype(x.dtype)


def _gqa_blocked(q, k, v):
    b, s, hkv, g, hd = q.shape
    nb = s // Q_BLOCK
    qb = q.reshape(b, nb, Q_BLOCK, hkv, g, hd).transpose(1, 0, 2, 3, 4, 5)

    def one_block(qblk):
        sc = jnp.einsum('bqkgd,bskd->bkgqs', qblk, k).astype(jnp.float32)
        p = jax.nn.softmax(sc, axis=-1).astype(v.dtype)
        return jnp.einsum('bkgqs,bskd->bqkgd', p, v)

    ob = lax.map(one_block, qb)
    return ob.transpose(1, 0, 2, 3, 4, 5).reshape(b, s, hkv * g * hd)


def _conformer_conv(a, gate, conv_dw, conv_b, conv_ln_g, conv_ln_b):
    h = a * jax.nn.sigmoid(gate)
    pad = (CONV_W - 1) // 2
    h = lax.conv_general_dilated(
        h, conv_dw.astype(h.dtype), window_strides=(1,), padding=((pad, pad),),
        dimension_numbers=('NWC', 'WIO', 'NWC'), feature_group_count=CONV_CH)
    h = h + conv_b.astype(h.dtype)
    h = _layernorm(h, conv_ln_g, conv_ln_b)
    return jax.nn.silu(h)


def _peer(xn, peer_wq, peer_keys, peer_u, peer_v):
    t, d = xn.shape
    xc_all = xn.reshape(t // TOKEN_CHUNK, TOKEN_CHUNK, d)
    half = PEER_DQ // 2

    def chunk(xc):
        q = (xc @ peer_wq).reshape(TOKEN_CHUNK, PEER_HEADS, PEER_DQ)
        s1 = jnp.einsum('chd,nd->chn', q[..., :half], peer_keys[0]).astype(jnp.float32)
        s2 = jnp.einsum('chd,nd->chn', q[..., half:], peer_keys[1]).astype(jnp.float32)
        v1, i1 = lax.top_k(s1, PEER_TOPK)
        v2, i2 = lax.top_k(s2, PEER_TOPK)
        cand_s = (v1[..., :, None] + v2[..., None, :]).reshape(TOKEN_CHUNK, PEER_HEADS, -1)
        cand_i = (i1[..., :, None] * N_KEYS + i2[..., None, :]).reshape(TOKEN_CHUNK, PEER_HEADS, -1)
        top_s, pos = lax.top_k(cand_s, PEER_TOPK)
        e = jnp.take_along_axis(cand_i, pos, axis=-1)
        g = jax.nn.softmax(top_s, axis=-1).astype(xc.dtype)
        u = peer_u[e]
        h = jax.nn.gelu(jnp.einsum('cd,chkd->chk', xc, u))
        return jnp.einsum('chk,chkd->cd', g * h, peer_v[e])

    return lax.map(chunk, xc_all).reshape(t, d)


def setup_inputs(seed: int = 0) -> dict:
    key = jax.random.key(seed)
    ks = jax.random.split(key, 16)
    f32 = jnp.float32
    nrm = lambda k, shp, sc: jax.random.normal(k, shp, f32) * sc
    return {
        "x": nrm(ks[0], (BATCH, SEQ, D_MODEL), 1.0),
        "norm1_g": 1.0 + nrm(ks[1], (D_MODEL,), 0.02),
        "w_in": nrm(ks[2], (D_MODEL, IN_WIDTH), D_MODEL ** -0.5),
        "q_norm_g": 1.0 + nrm(ks[3], (HEAD_DIM,), 0.02),
        "k_norm_g": 1.0 + nrm(ks[4], (HEAD_DIM,), 0.02),
        "conv_dw": nrm(ks[5], (CONV_W, 1, CONV_CH), CONV_W ** -0.5),
        "conv_b": nrm(ks[6], (CONV_CH,), 0.01),
        "conv_ln_g": 1.0 + nrm(ks[7], (CONV_CH,), 0.02),
        "conv_ln_b": nrm(ks[8], (CONV_CH,), 0.01),
        "w_out": nrm(ks[9], (MIX_WIDTH, D_MODEL), MIX_WIDTH ** -0.5),
        "norm2_g": 1.0 + nrm(ks[10], (D_MODEL,), 0.02),
        "peer_wq": nrm(ks[11], (D_MODEL, PEER_HEADS * PEER_DQ), D_MODEL ** -0.5),
        "peer_keys": nrm(ks[12], (2, N_KEYS, PEER_DQ // 2), (PEER_DQ // 2) ** -0.5),
        "peer_u": nrm(ks[13], (N_EXPERTS, D_MODEL), D_MODEL ** -0.5),
        "peer_v": nrm(ks[14], (N_EXPERTS, D_MODEL), 0.3),
        "final_g": 1.0 + nrm(ks[15], (D_MODEL,), 0.02),
    }


def reference(x, norm1_g, w_in, q_norm_g, k_norm_g, conv_dw, conv_b, conv_ln_g,
              conv_ln_b, w_out, norm2_g, peer_wq, peer_keys, peer_u, peer_v, final_g):
    b, s, d = x.shape
    rows = s // GRID_W
    row = jnp.repeat(jnp.arange(rows), GRID_W).astype(jnp.float32)
    col = jnp.tile(jnp.arange(GRID_W), rows).astype(jnp.float32)
    n_pairs = ROPE_AXIS_DIM // 2
    inv_freq = ROPE_THETA ** (-jnp.arange(n_pairs, dtype=jnp.float32) / n_pairs)
    ang_row = row[:, None] * inv_freq[None, :]
    ang_col = col[:, None] * inv_freq[None, :]
    group = N_HEADS // N_KV_HEADS

    for _ in range(DEPTH):
        h = _rmsnorm(x, norm1_g)
        p = h @ w_in
        o0 = ATTN_WIDTH
        o1 = o0 + KV_WIDTH
        o2 = o1 + KV_WIDTH
        o3 = o2 + CONV_CH
        q = p[..., :o0].reshape(b, s, N_HEADS, HEAD_DIM)
        k = p[..., o0:o1].reshape(b, s, N_KV_HEADS, HEAD_DIM)
        v = p[..., o1:o2].reshape(b, s, N_KV_HEADS, HEAD_DIM)
        q = _axial_rope(_rmsnorm(q, q_norm_g), ang_row, ang_col)
        k = _axial_rope(_rmsnorm(k, k_norm_g), ang_row, ang_col)
        q = (q * (HEAD_DIM ** -0.5)).reshape(b, s, N_KV_HEADS, group, HEAD_DIM)
        attn_out = _gqa_blocked(q, k, v)
        conv_out = _conformer_conv(p[..., o2:o3], p[..., o3:], conv_dw, conv_b,
                                   conv_ln_g, conv_ln_b)
        mixed = jnp.concatenate([attn_out, conv_out], axis=-1)
        x = x + mixed @ w_out
        hn = _rmsnorm(x, norm2_g).reshape(b * s, d)
        x = x + _peer(hn, peer_wq, peer_keys, peer_u, peer_v).reshape(b, s, d)

    return _rmsnorm(x, final_g)
```

```python
import functools

import jax
import jax.numpy as jnp
import numpy as np
from jax import lax
from jax.experimental import pallas as pl
from jax.experimental.pallas import tpu as pltpu

HEAD_DIM = 64
N_HEADS = 8
N_KV_HEADS = 2
GROUP = N_HEADS // N_KV_HEADS
ATTN_WIDTH = N_HEADS * HEAD_DIM
KV_WIDTH = N_KV_HEADS * HEAD_DIM
CONV_CH = 512
CONV_W = 31
CONV_PAD = (CONV_W - 1) // 2
GRID_W = 64
ROPE_THETA = 10000.0
ROPE_AXIS_DIM = HEAD_DIM // 2
N_KEYS = 128
PEER_HEADS = 8
PEER_DQ = 256
PEER_TOPK = 16
PEER_SEL = PEER_HEADS * PEER_TOPK
EPS = 1e-6

LANES = 128
SUBLANES = 8
HALO = 16
VMEM_LIMIT = 56 * 1024 * 1024

ROW_WORDS = 4
SLOT_STRIDE = 136

_NEG_INF = float("-inf")


def _cparams(sem, vmem=VMEM_LIMIT):
    return pltpu.CompilerParams(dimension_semantics=sem, vmem_limit_bytes=vmem)


def _group_mean_sq(v, ones_blockdiag):
    sq = v * v
    hi = sq.astype(jnp.bfloat16)
    lo = (sq - hi.astype(jnp.float32)).astype(jnp.bfloat16)
    tot = (jnp.dot(hi, ones_blockdiag, preferred_element_type=jnp.float32)
           + jnp.dot(lo, ones_blockdiag, preferred_element_type=jnp.float32))
    return tot * (1.0 / HEAD_DIM)


def _rope(v, cos, sin_signed, lane_in_pair_lo):
    width = v.shape[-1]
    up = pltpu.roll(v, width - ROPE_AXIS_DIM // 2, axis=1)
    down = pltpu.roll(v, ROPE_AXIS_DIM // 2, axis=1)
    partner = jnp.where(lane_in_pair_lo, up, down)
    return v * cos + partner * sin_signed


def _in_proj_kernel(x_ref, g1_ref, w_ref, qg_ref, kg_ref, cos_ref, sin_ref, ones_ref,
                    q_ref, kt_ref, v_ref, hg_ref):
    x = x_ref[...]
    ms = jnp.mean(x * x, axis=-1, keepdims=True)
    h = (x * lax.rsqrt(ms + EPS) * g1_ref[...]).astype(jnp.bfloat16)
    p = jnp.dot(h, w_ref[...], preferred_element_type=jnp.float32)
    o1 = ATTN_WIDTH
    o2 = o1 + KV_WIDTH
    o3 = o2 + KV_WIDTH
    o4 = o3 + CONV_CH
    q = p[:, :o1]
    k = p[:, o1:o2]
    v = p[:, o2:o3]
    a = p[:, o3:o4]
    gate = p[:, o4:]

    ones_bd = ones_ref[...]
    qn = q * lax.rsqrt(_group_mean_sq(q, ones_bd) + EPS) * qg_ref[...]
    kn = k * lax.rsqrt(_group_mean_sq(k, ones_bd[:KV_WIDTH, :KV_WIDTH]) + EPS) * kg_ref[...]

    cos2 = cos_ref[...]
    sin2 = sin_ref[...]
    cos_q = jnp.concatenate([cos2] * (ATTN_WIDTH // LANES), axis=1)
    sin_q = jnp.concatenate([sin2] * (ATTN_WIDTH // LANES), axis=1)
    lane_q = lax.broadcasted_iota(jnp.int32, q.shape, 1)
    lane_k = lax.broadcasted_iota(jnp.int32, k.shape, 1)
    half = ROPE_AXIS_DIM // 2
    qr = _rope(qn, cos_q, sin_q, (lane_q % ROPE_AXIS_DIM) < half) * (HEAD_DIM ** -0.5)
    kr = _rope(kn, cos2, sin2, (lane_k % ROPE_AXIS_DIM) < half)

    q_ref[...] = qr.astype(q_ref.dtype)
    kt_ref[0] = kr.T.astype(kt_ref.dtype)
    v_ref[...] = v.astype(v_ref.dtype)
    hg_ref[...] = a * (1.0 / (1.0 + jnp.exp(-gate)))


def _in_proj(x2, g1, w_in, qg, kg, cos_t, sin_t, ones_bd, batch, seq, tm):
    t, d = x2.shape
    n_s = seq // tm
    in_width = w_in.shape[1]
    return pl.pallas_call(
        _in_proj_kernel,
        grid=(t // tm,),
        in_specs=[
            pl.BlockSpec((tm, d), lambda i: (i, 0)),
            pl.BlockSpec((1, d), lambda i: (0, 0)),
            pl.BlockSpec((d, in_width), lambda i: (0, 0)),
            pl.BlockSpec((1, ATTN_WIDTH), lambda i: (0, 0)),
            pl.BlockSpec((1, KV_WIDTH), lambda i: (0, 0)),
            pl.BlockSpec((tm, LANES), lambda i: (i % n_s, 0)),
            pl.BlockSpec((tm, LANES), lambda i: (i % n_s, 0)),
            pl.BlockSpec((ATTN_WIDTH, ATTN_WIDTH), lambda i: (0, 0)),
        ],
        out_specs=[
            pl.BlockSpec((tm, ATTN_WIDTH), lambda i: (i, 0)),
            pl.BlockSpec((1, KV_WIDTH, tm), lambda i: (i // n_s, 0, i % n_s)),
            pl.BlockSpec((tm, KV_WIDTH), lambda i: (i, 0)),
            pl.BlockSpec((tm, CONV_CH), lambda i: (i, 0)),
        ],
        out_shape=[
            jax.ShapeDtypeStruct((t, ATTN_WIDTH), jnp.bfloat16),
            jax.ShapeDtypeStruct((batch, KV_WIDTH, seq), jnp.bfloat16),
            jax.ShapeDtypeStruct((t, KV_WIDTH), jnp.bfloat16),
            jax.ShapeDtypeStruct((t, CONV_CH), jnp.float32),
        ],
        compiler_params=_cparams(("parallel",)),
        name="in_proj",
    )(x2, g1, w_in, qg, kg, cos_t, sin_t, ones_bd)


def _attn_kernel(q_ref, kt_ref, v_ref, o_ref, m_ref, l_ref, acc_ref):
    ki = pl.program_id(2)

    @pl.when(ki == 0)
    def _():
        m_ref[...] = jnp.full(m_ref.shape, _NEG_INF, jnp.float32)
        l_ref[...] = jnp.zeros(l_ref.shape, jnp.float32)
        acc_ref[...] = jnp.zeros(acc_ref.shape, jnp.float32)

    for h in range(N_HEADS):
        g = h // GROUP
        qh = q_ref[:, h * HEAD_DIM:(h + 1) * HEAD_DIM]
        kt = kt_ref[0, g * HEAD_DIM:(g + 1) * HEAD_DIM, :]
        vv = v_ref[:, g * HEAD_DIM:(g + 1) * HEAD_DIM]
        s = jnp.dot(qh, kt, preferred_element_type=jnp.float32)
        m_prev = m_ref[h]
        m_new = jnp.maximum(m_prev, jnp.max(s, axis=-1, keepdims=True))
        alpha = jnp.exp(m_prev - m_new)
        p = jnp.exp(s - m_new)
        l_ref[h] = alpha * l_ref[h] + jnp.sum(p, axis=-1, keepdims=True)
        acc_ref[h] = alpha * acc_ref[h] + jnp.dot(p.astype(jnp.bfloat16), vv,
                                                  preferred_element_type=jnp.float32)
        m_ref[h] = m_new

    @pl.when(ki == pl.num_programs(2) - 1)
    def _():
        outs = [acc_ref[h] / l_ref[h] for h in range(N_HEADS)]
        o_ref[...] = jnp.concatenate(outs, axis=-1).astype(o_ref.dtype)


def _attention(q, kt, v, batch, seq, tq, tk):
    t = q.shape[0]
    nq = seq // tq
    nk = seq // tk
    return pl.pallas_call(
        _attn_kernel,
        grid=(batch, nq, nk),
        in_specs=[
            pl.BlockSpec((tq, ATTN_WIDTH), lambda b, qi, ki: (b * nq + qi, 0)),
            pl.BlockSpec((1, KV_WIDTH, tk), lambda b, qi, ki: (b, 0, ki)),
            pl.BlockSpec((tk, KV_WIDTH), lambda b, qi, ki: (b * nk + ki, 0)),
        ],
        out_specs=pl.BlockSpec((tq, ATTN_WIDTH), lambda b, qi, ki: (b * nq + qi, 0)),
        out_shape=jax.ShapeDtypeStruct((t, ATTN_WIDTH), jnp.bfloat16),
        scratch_shapes=[
            pltpu.VMEM((N_HEADS, tq, 1), jnp.float32),
            pltpu.VMEM((N_HEADS, tq, 1), jnp.float32),
            pltpu.VMEM((N_HEADS, tq, HEAD_DIM), jnp.float32),
        ],
        compiler_params=_cparams(("parallel", "parallel", "arbitrary")),
        name="attention",
    )(q, kt, v)


CONV_ROWS = 32


def _mix_out_kernel(attn_ref, hg_ref, prev_ref, next_ref, cw_ref, cb_ref, lg_ref, lb_ref,
                    wo_ref, x_ref, g2_ref, x1_ref, hn_ref, ext_ref, conv_ref, *, n_s):
    tm = hg_ref.shape[0]
    si = pl.program_id(0) % n_s
    keep_prev = (si > 0).astype(jnp.float32)
    keep_next = (si < n_s - 1).astype(jnp.float32)
    ext_ref[0:HALO, :] = prev_ref[...] * keep_prev
    ext_ref[HALO:HALO + tm, :] = hg_ref[...]
    ext_ref[HALO + tm:HALO + tm + HALO, :] = next_ref[...] * keep_next

    cw = cw_ref[...]
    cb = cb_ref[...]
    lg = lg_ref[...]
    lb = lb_ref[...]

    def conv_step(c, carry):
        start = pl.multiple_of(c * CONV_ROWS, CONV_ROWS)
        win = ext_ref[pl.ds(start, CONV_ROWS + 2 * HALO), :]
        acc = jnp.zeros((CONV_ROWS, CONV_CH), jnp.float32)
        for j in range(CONV_W):
            off = HALO - CONV_PAD + j
            acc = acc + win[off:off + CONV_ROWS, :] * cw[j:j + 1, :]
        acc = acc + cb
        mu = jnp.mean(acc, axis=-1, keepdims=True)
        cen = acc - mu
        var = jnp.mean(cen * cen, axis=-1, keepdims=True)
        y = cen * lax.rsqrt(var + EPS) * lg + lb
        y = y * (1.0 / (1.0 + jnp.exp(-y)))
        conv_ref[pl.ds(start, CONV_ROWS), :] = y.astype(conv_ref.dtype)
        return carry

    lax.fori_loop(0, tm // CONV_ROWS, conv_step, 0)

    mixed = (jnp.dot(attn_ref[...], wo_ref[0:ATTN_WIDTH, :], preferred_element_type=jnp.float32)
             + jnp.dot(conv_ref[...], wo_ref[ATTN_WIDTH:, :], preferred_element_type=jnp.float32))
    x1 = x_ref[...] + mixed
    x1_ref[...] = x1
    ms = jnp.mean(x1 * x1, axis=-1, keepdims=True)
    hn_ref[...] = x1 * lax.rsqrt(ms + EPS) * g2_ref[...]


def _mix_out(attn, hg, cw, cb, lg, lb, w_out, x2, g2, seq, tm):
    t, d = x2.shape
    n_s = seq // tm
    hb = tm // HALO
    last_halo = t // HALO - 1
    return pl.pallas_call(
        functools.partial(_mix_out_kernel, n_s=n_s),
        grid=(t // tm,),
        in_specs=[
            pl.BlockSpec((tm, ATTN_WIDTH), lambda i: (i, 0)),
            pl.BlockSpec((tm, CONV_CH), lambda i: (i, 0)),
            pl.BlockSpec((HALO, CONV_CH), lambda i: (jnp.maximum(i * hb - 1, 0), 0)),
            pl.BlockSpec((HALO, CONV_CH), lambda i: (jnp.minimum((i + 1) * hb, last_halo), 0)),
            pl.BlockSpec((CONV_W + 1, CONV_CH), lambda i: (0, 0)),
            pl.BlockSpec((1, CONV_CH), lambda i: (0, 0)),
            pl.BlockSpec((1, CONV_CH), lambda i: (0, 0)),
            pl.BlockSpec((1, CONV_CH), lambda i: (0, 0)),
            pl.BlockSpec((d, d), lambda i: (0, 0)),
            pl.BlockSpec((tm, d), lambda i: (i, 0)),
            pl.BlockSpec((1, d), lambda i: (0, 0)),
        ],
        out_specs=[
            pl.BlockSpec((tm, d), lambda i: (i, 0)),
            pl.BlockSpec((tm, d), lambda i: (i, 0)),
        ],
        out_shape=[
            jax.ShapeDtypeStruct((t, d), jnp.float32),
            jax.ShapeDtypeStruct((t, d), jnp.float32),
        ],
        scratch_shapes=[
            pltpu.VMEM((tm + 2 * HALO, CONV_CH), jnp.float32),
            pltpu.VMEM((tm, CONV_CH), jnp.bfloat16),
        ],
        compiler_params=_cparams(("parallel",)),
        name="mix_out",
    )(attn, hg, hg, hg, cw, cb, lg, lb, w_out, x2, g2)


def _top_rows(s, order, payload, count):
    big = jnp.int32(2 ** 30)
    vals = []
    picks = []
    for _ in range(count):
        m = jnp.max(s, axis=0, keepdims=True)
        first = jnp.min(jnp.where(s == m, order, big), axis=0, keepdims=True)
        chosen = order == first
        if payload is order:
            pick = first
        else:
            pick = jnp.max(jnp.where(chosen, payload, -1), axis=0, keepdims=True)
        vals.append(m)
        picks.append(pick)
        s = jnp.where(chosen, _NEG_INF, s)
    return jnp.concatenate(vals, axis=0), jnp.concatenate(picks, axis=0)


def _candidate_blocks():
    blocks = [(0, 0, 16)]
    for a in range(1, 8):
        blocks.append((a, 0, 8))
    return blocks


def _route_kernel(hn_ref, wqt_ref, keys_ref, e_ref, g_ref):
    hn = hn_ref[...].astype(jnp.bfloat16)
    qpt = lax.dot_general(wqt_ref[...], hn, (((1,), (1,)), ((), ())),
                          preferred_element_type=jnp.float32)
    qpt = qpt.astype(jnp.bfloat16)
    tm = hn.shape[0]
    half = PEER_DQ // 2
    k0 = keys_ref[0]
    k1 = keys_ref[1]
    row_iota = lax.broadcasted_iota(jnp.int32, (N_KEYS, LANES), 0)

    blocks = _candidate_blocks()
    flat_parts = []
    for a, b0, nb in blocks:
        flat_parts.append(a * PEER_TOPK + b0 + lax.broadcasted_iota(jnp.int32, (nb, LANES), 0))
    tail_flat = (8 + lax.broadcasted_iota(jnp.int32, (8, LANES), 0)) * PEER_TOPK
    flat = jnp.concatenate(flat_parts + [tail_flat], axis=0)

    for h in range(PEER_HEADS):
        q1 = qpt[h * PEER_DQ:h * PEER_DQ + half, :]
        q2 = qpt[h * PEER_DQ + half:(h + 1) * PEER_DQ, :]
        s1 = jnp.dot(k0, q1, preferred_element_type=jnp.float32)
        s2 = jnp.dot(k1, q2, preferred_element_type=jnp.float32)
        for c in range(tm // LANES):
            sl = slice(c * LANES, (c + 1) * LANES)
            v1, i1 = _top_rows(s1[:, sl], row_iota, row_iota, PEER_TOPK)
            v2, i2 = _top_rows(s2[:, sl], row_iota, row_iota, PEER_TOPK)
            cs = []
            ce = []
            for a, b0, nb in blocks:
                cs.append(v1[a:a + 1, :] + v2[b0:b0 + nb, :])
                ce.append(i1[a:a + 1, :] * N_KEYS + i2[b0:b0 + nb, :])
            cs.append(v1[8:16, :] + v2[0:1, :])
            ce.append(i1[8:16, :] * N_KEYS + i2[0:1, :])
            cand_s = jnp.concatenate(cs, axis=0)
            cand_e = jnp.concatenate(ce, axis=0)
            top_s, top_e = _top_rows(cand_s, flat, cand_e, PEER_TOPK)
            ex = jnp.exp(top_s - top_s[0:1, :])
            gates = ex / jnp.sum(ex, axis=0, keepdims=True)
            e_ref[h, :, sl] = top_e
            g_ref[h, :, sl] = gates


def _route(hn, wqt, keys, tm):
    t, d = hn.shape
    return pl.pallas_call(
        _route_kernel,
        grid=(t // tm,),
        in_specs=[
            pl.BlockSpec((tm, d), lambda i: (i, 0)),
            pl.BlockSpec(wqt.shape, lambda i: (0, 0)),
            pl.BlockSpec(keys.shape, lambda i: (0, 0, 0)),
        ],
        out_specs=[
            pl.BlockSpec((PEER_HEADS, PEER_TOPK, tm), lambda i: (0, 0, i)),
            pl.BlockSpec((PEER_HEADS, PEER_TOPK, tm), lambda i: (0, 0, i)),
        ],
        out_shape=[
            jax.ShapeDtypeStruct((PEER_HEADS, PEER_TOPK, t), jnp.int32),
            jax.ShapeDtypeStruct((PEER_HEADS, PEER_TOPK, t), jnp.float32),
        ],
        compiler_params=_cparams(("parallel",)),
        name="route",
    )(hn, wqt, keys)


PEER_TOK = 128
IDX_COLS = PEER_TOK * PEER_SEL // SUBLANES
TOK_PER_IDX_ROW = IDX_COLS // PEER_SEL


def _unpack_words(w):
    lo = pltpu.bitcast(w << 16, jnp.float32)
    hi = pltpu.bitcast(w & jnp.int32(-65536), jnp.float32)
    return lo, hi


def _gelu_tanh(x):
    c = np.float32(np.sqrt(2.0 / np.pi))
    return 0.5 * x * (1.0 + jnp.tanh(c * (x + 0.044715 * (x * x * x))))


def _peer_u_kernel(e_ref, x_ref, gate_ref, tbl_ref, w_ref, slot_ref, s_ref):
    lane = lax.broadcasted_iota(jnp.int32, (PEER_SEL, PEER_TOK), 1)

    def token(t, carry):
        irow = t // TOK_PER_IDX_ROW
        icol = (t % TOK_PER_IDX_ROW) * PEER_SEL
        for k in range(PEER_SEL):
            e = e_ref[irow, icol + k]
            row = tbl_ref[pl.ds(pl.multiple_of(e * ROW_WORDS, ROW_WORDS), ROW_WORDS), :]
            slot_ref[pl.ds(k, ROW_WORDS, stride=SLOT_STRIDE), :] = row
        xt = x_ref[t]
        acc = jnp.zeros((PEER_SEL, LANES), jnp.float32)
        for r in range(ROW_WORDS):
            lo, hi = _unpack_words(slot_ref[r * SLOT_STRIDE:r * SLOT_STRIDE + PEER_SEL, :])
            acc = acc + lo * xt[r:r + 1, :] + hi * xt[ROW_WORDS + r:ROW_WORDS + r + 1, :]
        s = jnp.sum(acc, axis=1, keepdims=True)
        s_ref[...] = jnp.where(lane == t, s, s_ref[...])
        return carry

    lax.fori_loop(0, PEER_TOK, token, 0)
    gates = gate_ref[...].reshape(PEER_SEL, PEER_TOK)
    w_ref[...] = gates * _gelu_tanh(s_ref[...])


def _peer_u(e_sm, x3, gates, tbl):
    t = x3.shape[0]
    return pl.pallas_call(
        _peer_u_kernel,
        grid=(t // PEER_TOK,),
        in_specs=[
            pl.BlockSpec((SUBLANES, IDX_COLS), lambda i: (i, 0), memory_space=pltpu.SMEM),
            pl.BlockSpec((PEER_TOK, SUBLANES, LANES), lambda i: (i, 0, 0)),
            pl.BlockSpec((PEER_HEADS, PEER_TOPK, PEER_TOK), lambda i: (0, 0, i)),
            pl.BlockSpec(tbl.shape, lambda i: (0, 0), pipeline_mode=pl.Buffered(1)),
        ],
        out_specs=pl.BlockSpec((PEER_SEL, PEER_TOK), lambda i: (0, i)),
        out_shape=jax.ShapeDtypeStruct((PEER_SEL, t), jnp.float32),
        scratch_shapes=[
            pltpu.VMEM((ROW_WORDS * SLOT_STRIDE, LANES), jnp.int32),
            pltpu.VMEM((PEER_SEL, PEER_TOK), jnp.float32),
        ],
        compiler_params=_cparams(("arbitrary",)),
        name="peer_u",
    )(e_sm, x3, gates, tbl)


def _peer_v_kernel(e_ref, w_ref, x1_ref, fg_ref, tbl_ref, y_ref):
    fg = fg_ref[...]
    d_model = SUBLANES * LANES

    def token(t, carry):
        irow = t // TOK_PER_IDX_ROW
        icol = (t % TOK_PER_IDX_ROW) * PEER_SEL
        acc_lo = jnp.zeros((ROW_WORDS, LANES), jnp.float32)
        acc_hi = jnp.zeros((ROW_WORDS, LANES), jnp.float32)
        for k in range(PEER_SEL):
            e = e_ref[irow, icol + k]
            wk = w_ref[irow, icol + k]
            row = tbl_ref[pl.ds(pl.multiple_of(e * ROW_WORDS, ROW_WORDS), ROW_WORDS), :]
            lo, hi = _unpack_words(row)
            acc_lo = acc_lo + wk * lo
            acc_hi = acc_hi + wk * hi
        x2 = x1_ref[t] + jnp.concatenate([acc_lo, acc_hi], axis=0)
        ms = jnp.sum(x2 * x2, axis=(0, 1), keepdims=True) * (1.0 / d_model)
        y_ref[t] = x2 * lax.rsqrt(ms + EPS) * fg
        return carry

    lax.fori_loop(0, PEER_TOK, token, 0)


def _peer_v(e_sm, w_sm, x13, fg, tbl):
    t = x13.shape[0]
    return pl.pallas_call(
        _peer_v_kernel,
        grid=(t // PEER_TOK,),
        in_specs=[
            pl.BlockSpec((SUBLANES, IDX_COLS), lambda i: (i, 0), memory_space=pltpu.SMEM),
            pl.BlockSpec((SUBLANES, IDX_COLS), lambda i: (i, 0), memory_space=pltpu.SMEM),
            pl.BlockSpec((PEER_TOK, SUBLANES, LANES), lambda i: (i, 0, 0)),
            pl.BlockSpec((SUBLANES, LANES), lambda i: (0, 0)),
            pl.BlockSpec(tbl.shape, lambda i: (0, 0), pipeline_mode=pl.Buffered(1)),
        ],
        out_specs=pl.BlockSpec((PEER_TOK, SUBLANES, LANES), lambda i: (i, 0, 0)),
        out_shape=jax.ShapeDtypeStruct((t, SUBLANES, LANES), jnp.float32),
        compiler_params=_cparams(("arbitrary",)),
        name="peer_v",
    )(e_sm, w_sm, x13, fg, tbl)


def _pack_table(tbl):
    n, d = tbl.shape
    bits = lax.bitcast_convert_type(tbl.astype(jnp.bfloat16), jnp.uint16).astype(jnp.uint32)
    words = bits[:, :d // 2] | (bits[:, d // 2:] << 16)
    return lax.bitcast_convert_type(words, jnp.int32).reshape(n * ROW_WORDS, LANES)


def _rope_tables(seq):
    pos = jnp.arange(seq)
    row = (pos // GRID_W).astype(jnp.float32)
    col = (pos % GRID_W).astype(jnp.float32)
    n_pairs = ROPE_AXIS_DIM // 2
    inv_freq = ROPE_THETA ** (-jnp.arange(n_pairs, dtype=jnp.float32) / n_pairs)
    ang_r = row[:, None] * inv_freq[None, :]
    ang_c = col[:, None] * inv_freq[None, :]
    cos = jnp.concatenate([jnp.cos(ang_r), jnp.cos(ang_r), jnp.cos(ang_c), jnp.cos(ang_c)], axis=1)
    sin = jnp.concatenate([-jnp.sin(ang_r), jnp.sin(ang_r), -jnp.sin(ang_c), jnp.sin(ang_c)], axis=1)
    reps = LANES // HEAD_DIM
    return jnp.tile(cos, (1, reps)), jnp.tile(sin, (1, reps))


def kernel(x, norm1_g, w_in, q_norm_g, k_norm_g, conv_dw, conv_b, conv_ln_g, conv_ln_b, w_out,
           norm2_g, peer_wq, peer_keys, peer_u, peer_v, final_g):
    b, s, d = x.shape
    t = b * s
    assert d == SUBLANES * LANES and s % GRID_W == 0
    tm = min(512, s)
    tq = min(512, s)
    tk = min(512, s)
    tr = min(256, s)
    assert s % tm == 0 and t % PEER_TOK == 0

    f32 = jnp.float32
    bf16 = jnp.bfloat16
    x2 = x.reshape(t, d)
    cos_t, sin_t = _rope_tables(s)
    head_id = jnp.arange(ATTN_WIDTH) // HEAD_DIM
    ones_bd = (head_id[:, None] == head_id[None, :]).astype(bf16)

    q, kt, v, hg = _in_proj(
        x2, norm1_g.reshape(1, d), w_in.astype(bf16),
        jnp.tile(q_norm_g, N_HEADS).reshape(1, ATTN_WIDTH),
        jnp.tile(k_norm_g, N_KV_HEADS).reshape(1, KV_WIDTH),
        cos_t, sin_t, ones_bd, b, s, tm)

    attn = _attention(q, kt, v, b, s, tq, tk)

    cw = jnp.concatenate([conv_dw.reshape(CONV_W, CONV_CH), jnp.zeros((1, CONV_CH), f32)], axis=0)
    x1, hn = _mix_out(attn, hg, cw, conv_b.reshape(1, CONV_CH), conv_ln_g.reshape(1, CONV_CH),
                      conv_ln_b.reshape(1, CONV_CH), w_out.astype(bf16), x2,
                      norm2_g.reshape(1, d), s, tm)

    e, gates = _route(hn, peer_wq.T.astype(bf16), peer_keys.astype(bf16), tr)

    n_blk = t // PEER_TOK
    e_sm = e.reshape(PEER_SEL, t).T.reshape(n_blk * SUBLANES, IDX_COLS)
    w = _peer_u(e_sm, hn.reshape(t, SUBLANES, LANES), gates, _pack_table(peer_u))
    w_sm = w.T.reshape(n_blk * SUBLANES, IDX_COLS)
    y = _peer_v(e_sm, w_sm, x1.reshape(t, SUBLANES, LANES), final_g.reshape(SUBLANES, LANES),
                _pack_table(peer_v))
    return y.reshape(b, s, d)
```

```python
import functools

import jax
import jax.numpy as jnp
import numpy as np
from jax import lax
from jax.experimental import pallas as pl
from jax.experimental.pallas import tpu as pltpu

HEAD_DIM = 64
N_HEADS = 8
N_KV_HEADS = 2
GROUP = N_HEADS // N_KV_HEADS
ATTN_WIDTH = N_HEADS * HEAD_DIM
KV_WIDTH = N_KV_HEADS * HEAD_DIM
CONV_CH = 512
CONV_W = 31
CONV_PAD = (CONV_W - 1) // 2
GRID_W = 64
ROPE_THETA = 10000.0
ROPE_AXIS_DIM = HEAD_DIM // 2
N_KEYS = 128
PEER_HEADS = 8
PEER_DQ = 256
PEER_TOPK = 16
PEER_SEL = PEER_HEADS * PEER_TOPK
EPS = 1e-6

LANES = 128
SUBLANES = 8
HALO = 16
VMEM_LIMIT = 56 * 1024 * 1024

ROW_TILE = 8

_NEG_INF = float("-inf")


def _cparams(sem, vmem=VMEM_LIMIT):
    return pltpu.CompilerParams(dimension_semantics=sem, vmem_limit_bytes=vmem)


def _group_mean_sq(v, ones_blockdiag):
    sq = v * v
    hi = sq.astype(jnp.bfloat16)
    lo = (sq - hi.astype(jnp.float32)).astype(jnp.bfloat16)
    tot = (jnp.dot(hi, ones_blockdiag, preferred_element_type=jnp.float32)
           + jnp.dot(lo, ones_blockdiag, preferred_element_type=jnp.float32))
    return tot * (1.0 / HEAD_DIM)


def _rope(v, cos, sin_signed, lane_in_pair_lo):
    width = v.shape[-1]
    up = pltpu.roll(v, width - ROPE_AXIS_DIM // 2, axis=1)
    down = pltpu.roll(v, ROPE_AXIS_DIM // 2, axis=1)
    partner = jnp.where(lane_in_pair_lo, up, down)
    return v * cos + partner * sin_signed


def _in_proj_kernel(x_ref, g1_ref, w_ref, qg_ref, kg_ref, cos_ref, sin_ref, ones_ref,
                    qt_ref, k_ref, vt_ref, hg_ref):
    x = x_ref[...]
    ms = jnp.mean(x * x, axis=-1, keepdims=True)
    h = (x * lax.rsqrt(ms + EPS) * g1_ref[...]).astype(jnp.bfloat16)
    p = jnp.dot(h, w_ref[...], preferred_element_type=jnp.float32)
    o1 = ATTN_WIDTH
    o2 = o1 + KV_WIDTH
    o3 = o2 + KV_WIDTH
    o4 = o3 + CONV_CH
    q = p[:, :o1]
    k = p[:, o1:o2]
    v = p[:, o2:o3]
    a = p[:, o3:o4]
    gate = p[:, o4:]

    ones_bd = ones_ref[...]
    qn = q * lax.rsqrt(_group_mean_sq(q, ones_bd) + EPS) * qg_ref[...]
    kn = k * lax.rsqrt(_group_mean_sq(k, ones_bd[:KV_WIDTH, :KV_WIDTH]) + EPS) * kg_ref[...]

    cos2 = cos_ref[...]
    sin2 = sin_ref[...]
    cos_q = jnp.concatenate([cos2] * (ATTN_WIDTH // LANES), axis=1)
    sin_q = jnp.concatenate([sin2] * (ATTN_WIDTH // LANES), axis=1)
    lane_q = lax.broadcasted_iota(jnp.int32, q.shape, 1)
    lane_k = lax.broadcasted_iota(jnp.int32, k.shape, 1)
    half = ROPE_AXIS_DIM // 2
    qr = _rope(qn, cos_q, sin_q, (lane_q % ROPE_AXIS_DIM) < half) * (HEAD_DIM ** -0.5)
    kr = _rope(kn, cos2, sin2, (lane_k % ROPE_AXIS_DIM) < half)

    qt = qr.T.astype(qt_ref.dtype)
    zeros = jnp.zeros((HEAD_DIM, qt.shape[1]), qt_ref.dtype)
    for hh in range(N_HEADS):
        g = hh // GROUP
        piece = qt[hh * HEAD_DIM:(hh + 1) * HEAD_DIM, :]
        qt_ref[0, hh, g * HEAD_DIM:(g + 1) * HEAD_DIM, :] = piece
        qt_ref[0, hh, (1 - g) * HEAD_DIM:(2 - g) * HEAD_DIM, :] = zeros
    k_ref[...] = kr.astype(k_ref.dtype)
    vt_ref[0] = v.T.astype(vt_ref.dtype)
    hg_ref[...] = a * (1.0 / (1.0 + jnp.exp(-gate)))


def _in_proj(x2, g1, w_in, qg, kg, cos_t, sin_t, ones_bd, batch, seq, tm):
    t, d = x2.shape
    n_s = seq // tm
    in_width = w_in.shape[1]
    return pl.pallas_call(
        _in_proj_kernel,
        grid=(t // tm,),
        in_specs=[
            pl.BlockSpec((tm, d), lambda i: (i, 0)),
            pl.BlockSpec((1, d), lambda i: (0, 0)),
            pl.BlockSpec((d, in_width), lambda i: (0, 0)),
            pl.BlockSpec((1, ATTN_WIDTH), lambda i: (0, 0)),
            pl.BlockSpec((1, KV_WIDTH), lambda i: (0, 0)),
            pl.BlockSpec((tm, LANES), lambda i: (i % n_s, 0)),
            pl.BlockSpec((tm, LANES), lambda i: (i % n_s, 0)),
            pl.BlockSpec((ATTN_WIDTH, ATTN_WIDTH), lambda i: (0, 0)),
        ],
        out_specs=[
            pl.BlockSpec((1, N_HEADS, KV_WIDTH, tm), lambda i: (i // n_s, 0, 0, i % n_s)),
            pl.BlockSpec((tm, KV_WIDTH), lambda i: (i, 0)),
            pl.BlockSpec((1, KV_WIDTH, tm), lambda i: (i // n_s, 0, i % n_s)),
            pl.BlockSpec((tm, CONV_CH), lambda i: (i, 0)),
        ],
        out_shape=[
            jax.ShapeDtypeStruct((batch, N_HEADS, KV_WIDTH, seq), jnp.bfloat16),
            jax.ShapeDtypeStruct((t, KV_WIDTH), jnp.bfloat16),
            jax.ShapeDtypeStruct((batch, KV_WIDTH, seq), jnp.bfloat16),
            jax.ShapeDtypeStruct((t, CONV_CH), jnp.float32),
        ],
        compiler_params=_cparams(("parallel",)),
        name="in_proj",
    )(x2, g1, w_in, qg, kg, cos_t, sin_t, ones_bd)


def _attn_kernel(qt_ref, k_ref, vt_ref, o_ref, m_ref, l_ref, acc_ref):
    ki = pl.program_id(2)

    @pl.when(ki == 0)
    def _():
        m_ref[...] = jnp.full(m_ref.shape, _NEG_INF, jnp.float32)
        l_ref[...] = jnp.zeros(l_ref.shape, jnp.float32)
        acc_ref[...] = jnp.zeros(acc_ref.shape, jnp.float32)

    k = k_ref[...]
    for h in range(N_HEADS):
        g = h // GROUP
        s = jnp.dot(k, qt_ref[0, h], preferred_element_type=jnp.float32)
        m_prev = m_ref[h:h + 1, :]
        m_new = jnp.maximum(m_prev, jnp.max(s, axis=0, keepdims=True))
        alpha = jnp.exp(m_prev - m_new)
        p = jnp.exp(s - m_new)
        l_ref[h:h + 1, :] = alpha * l_ref[h:h + 1, :] + jnp.sum(p, axis=0, keepdims=True)
        pv = jnp.dot(vt_ref[0, g * HEAD_DIM:(g + 1) * HEAD_DIM, :], p.astype(jnp.bfloat16),
                     preferred_element_type=jnp.float32)
        acc_ref[h] = alpha * acc_ref[h] + pv
        m_ref[h:h + 1, :] = m_new

    @pl.when(ki == pl.num_programs(2) - 1)
    def _():
        outs = [acc_ref[h] / l_ref[h:h + 1, :] for h in range(N_HEADS)]
        o_ref[...] = jnp.concatenate(outs, axis=0).T.astype(o_ref.dtype)


def _attention(qt, k, vt, batch, seq, tq, tk):
    t = k.shape[0]
    nq = seq // tq
    nk = seq // tk
    return pl.pallas_call(
        _attn_kernel,
        grid=(batch, nq, nk),
        in_specs=[
            pl.BlockSpec((1, N_HEADS, KV_WIDTH, tq), lambda b, qi, ki: (b, 0, 0, qi)),
            pl.BlockSpec((tk, KV_WIDTH), lambda b, qi, ki: (b * nk + ki, 0)),
            pl.BlockSpec((1, KV_WIDTH, tk), lambda b, qi, ki: (b, 0, ki)),
        ],
        out_specs=pl.BlockSpec((tq, ATTN_WIDTH), lambda b, qi, ki: (b * nq + qi, 0)),
        out_shape=jax.ShapeDtypeStruct((t, ATTN_WIDTH), jnp.bfloat16),
        scratch_shapes=[
            pltpu.VMEM((N_HEADS, tq), jnp.float32),
            pltpu.VMEM((N_HEADS, tq), jnp.float32),
            pltpu.VMEM((N_HEADS, HEAD_DIM, tq), jnp.float32),
        ],
        compiler_params=_cparams(("parallel", "parallel", "arbitrary")),
        name="attention",
    )(qt, k, vt)


CONV_ROWS = 32


def _mix_out_kernel(attn_ref, hg_ref, prev_ref, next_ref, cw_ref, cb_ref, lg_ref, lb_ref,
                    wo_ref, x_ref, g2_ref, x1_ref, hn_ref, ext_ref, conv_ref, *, n_s):
    tm = hg_ref.shape[0]
    si = pl.program_id(0) % n_s
    keep_prev = (si > 0).astype(jnp.float32)
    keep_next = (si < n_s - 1).astype(jnp.float32)
    ext_ref[0:HALO, :] = prev_ref[...] * keep_prev
    ext_ref[HALO:HALO + tm, :] = hg_ref[...]
    ext_ref[HALO + tm:HALO + tm + HALO, :] = next_ref[...] * keep_next

    cw = cw_ref[...]
    cb = cb_ref[...]
    lg = lg_ref[...]
    lb = lb_ref[...]

    def conv_step(c, carry):
        start = pl.multiple_of(c * CONV_ROWS, CONV_ROWS)
        win = ext_ref[pl.ds(start, CONV_ROWS + 2 * HALO), :]
        acc = jnp.zeros((CONV_ROWS, CONV_CH), jnp.float32)
        for j in range(CONV_W):
            off = HALO - CONV_PAD + j
            acc = acc + win[off:off + CONV_ROWS, :] * cw[j:j + 1, :]
        acc = acc + cb
        mu = jnp.mean(acc, axis=-1, keepdims=True)
        cen = acc - mu
        var = jnp.mean(cen * cen, axis=-1, keepdims=True)
        y = cen * lax.rsqrt(var + EPS) * lg + lb
        y = y * (1.0 / (1.0 + jnp.exp(-y)))
        conv_ref[pl.ds(start, CONV_ROWS), :] = y.astype(conv_ref.dtype)
        return carry

    lax.fori_loop(0, tm // CONV_ROWS, conv_step, 0)

    mixed = (jnp.dot(attn_ref[...], wo_ref[0:ATTN_WIDTH, :], preferred_element_type=jnp.float32)
             + jnp.dot(conv_ref[...], wo_ref[ATTN_WIDTH:, :], preferred_element_type=jnp.float32))
    x1 = x_ref[...] + mixed
    x1_ref[...] = x1
    ms = jnp.mean(x1 * x1, axis=-1, keepdims=True)
    hn_ref[...] = x1 * lax.rsqrt(ms + EPS) * g2_ref[...]


def _mix_out(attn, hg, cw, cb, lg, lb, w_out, x2, g2, seq, tm):
    t, d = x2.shape
    n_s = seq // tm
    hb = tm // HALO
    last_halo = t // HALO - 1
    return pl.pallas_call(
        functools.partial(_mix_out_kernel, n_s=n_s),
        grid=(t // tm,),
        in_specs=[
            pl.BlockSpec((tm, ATTN_WIDTH), lambda i: (i, 0)),
            pl.BlockSpec((tm, CONV_CH), lambda i: (i, 0)),
            pl.BlockSpec((HALO, CONV_CH), lambda i: (jnp.maximum(i * hb - 1, 0), 0)),
            pl.BlockSpec((HALO, CONV_CH), lambda i: (jnp.minimum((i + 1) * hb, last_halo), 0)),
            pl.BlockSpec((CONV_W + 1, CONV_CH), lambda i: (0, 0)),
            pl.BlockSpec((1, CONV_CH), lambda i: (0, 0)),
            pl.BlockSpec((1, CONV_CH), lambda i: (0, 0)),
            pl.BlockSpec((1, CONV_CH), lambda i: (0, 0)),
            pl.BlockSpec((d, d), lambda i: (0, 0)),
            pl.BlockSpec((tm, d), lambda i: (i, 0)),
            pl.BlockSpec((1, d), lambda i: (0, 0)),
        ],
        out_specs=[
            pl.BlockSpec((tm, d), lambda i: (i, 0)),
            pl.BlockSpec((tm, d), lambda i: (i, 0)),
        ],
        out_shape=[
            jax.ShapeDtypeStruct((t, d), jnp.float32),
            jax.ShapeDtypeStruct((t, d), jnp.float32),
        ],
        scratch_shapes=[
            pltpu.VMEM((tm + 2 * HALO, CONV_CH), jnp.float32),
            pltpu.VMEM((tm, CONV_CH), jnp.bfloat16),
        ],
        compiler_params=_cparams(("parallel",)),
        name="mix_out",
    )(attn, hg, hg, hg, cw, cb, lg, lb, w_out, x2, g2)


def _top_rows(s, order, payload, count):
    big = jnp.int32(2 ** 30)
    vals = []
    picks = []
    for _ in range(count):
        m = jnp.max(s, axis=0, keepdims=True)
        first = jnp.min(jnp.where(s == m, order, big), axis=0, keepdims=True)
        chosen = order == first
        if payload is order:
            pick = first
        else:
            pick = jnp.max(jnp.where(chosen, payload, -1), axis=0, keepdims=True)
        vals.append(m)
        picks.append(pick)
        s = jnp.where(chosen, _NEG_INF, s)
    return jnp.concatenate(vals, axis=0), jnp.concatenate(picks, axis=0)


def _candidate_blocks():
    blocks = [(0, 0, 16)]
    for a in range(1, 8):
        blocks.append((a, 0, 8))
    return blocks


def _route_kernel(hn_ref, wqt_ref, keys_ref, e_ref, g_ref):
    hn = hn_ref[...].astype(jnp.bfloat16)
    qpt = lax.dot_general(wqt_ref[...], hn, (((1,), (1,)), ((), ())),
                          preferred_element_type=jnp.float32)
    qpt = qpt.astype(jnp.bfloat16)
    tm = hn.shape[0]
    half = PEER_DQ // 2
    k0 = keys_ref[0]
    k1 = keys_ref[1]
    row_iota = lax.broadcasted_iota(jnp.int32, (N_KEYS, LANES), 0)

    blocks = _candidate_blocks()
    flat_parts = []
    for a, b0, nb in blocks:
        flat_parts.append(a * PEER_TOPK + b0 + lax.broadcasted_iota(jnp.int32, (nb, LANES), 0))
    tail_flat = (8 + lax.broadcasted_iota(jnp.int32, (8, LANES), 0)) * PEER_TOPK
    flat = jnp.concatenate(flat_parts + [tail_flat], axis=0)

    for h in range(PEER_HEADS):
        q1 = qpt[h * PEER_DQ:h * PEER_DQ + half, :]
        q2 = qpt[h * PEER_DQ + half:(h + 1) * PEER_DQ, :]
        s1 = jnp.dot(k0, q1, preferred_element_type=jnp.float32)
        s2 = jnp.dot(k1, q2, preferred_element_type=jnp.float32)
        for c in range(tm // LANES):
            sl = slice(c * LANES, (c + 1) * LANES)
            v1, i1 = _top_rows(s1[:, sl], row_iota, row_iota, PEER_TOPK)
            v2, i2 = _top_rows(s2[:, sl], row_iota, row_iota, PEER_TOPK)
            cs = []
            ce = []
            for a, b0, nb in blocks:
                cs.append(v1[a:a + 1, :] + v2[b0:b0 + nb, :])
                ce.append(i1[a:a + 1, :] * N_KEYS + i2[b0:b0 + nb, :])
            cs.append(v1[8:16, :] + v2[0:1, :])
            ce.append(i1[8:16, :] * N_KEYS + i2[0:1, :])
            cand_s = jnp.concatenate(cs, axis=0)
            cand_e = jnp.concatenate(ce, axis=0)
            top_s, top_e = _top_rows(cand_s, flat, cand_e, PEER_TOPK)
            ex = jnp.exp(top_s - top_s[0:1, :])
            gates = ex / jnp.sum(ex, axis=0, keepdims=True)
            e_ref[h, :, sl] = top_e
            g_ref[h, :, sl] = gates


def _route(hn, wqt, keys, tm):
    t, d = hn.shape
    return pl.pallas_call(
        _route_kernel,
        grid=(t // tm,),
        in_specs=[
            pl.BlockSpec((tm, d), lambda i: (i, 0)),
            pl.BlockSpec(wqt.shape, lambda i: (0, 0)),
            pl.BlockSpec(keys.shape, lambda i: (0, 0, 0)),
        ],
        out_specs=[
            pl.BlockSpec((PEER_HEADS, PEER_TOPK, tm), lambda i: (0, 0, i)),
            pl.BlockSpec((PEER_HEADS, PEER_TOPK, tm), lambda i: (0, 0, i)),
        ],
        out_shape=[
            jax.ShapeDtypeStruct((PEER_HEADS, PEER_TOPK, t), jnp.int32),
            jax.ShapeDtypeStruct((PEER_HEADS, PEER_TOPK, t), jnp.float32),
        ],
        compiler_params=_cparams(("parallel",)),
        name="route",
    )(hn, wqt, keys)


PEER_TOK = 128


def _expert_row(tbl_ref, e):
    return tbl_ref[e].astype(jnp.float32)


def _gelu_tanh(x):
    c = np.float32(np.sqrt(2.0 / np.pi))
    return 0.5 * x * (1.0 + jnp.tanh(c * (x + 0.044715 * (x * x * x))))


def _fold_rows(rows, x, x_swapped, masks):
    lo4, mod4_lo2, even = masks
    a, e, c, g, b, f, d, h = rows

    def level1(p, q):
        t1 = jnp.where(lo4, p, q)
        t2 = pltpu.roll(jnp.where(lo4, q, p), 4, axis=0)
        return t1 * x + t2 * x_swapped

    def level2(u, v):
        uu = u + pltpu.roll(u, 6, axis=0)
        vv = v + pltpu.roll(v, 2, axis=0)
        return jnp.where(mod4_lo2, uu, vv)

    def level3(u, v):
        uu = u + pltpu.roll(u, 7, axis=0)
        vv = v + pltpu.roll(v, 1, axis=0)
        return jnp.where(even, uu, vv)

    return level3(level2(level1(a, b), level1(c, d)), level2(level1(e, f), level1(g, h)))


def _peer_u_kernel(e_ref, x_ref, gate_ref, tbl_ref, w_ref, s_ref, part_a, part_b):
    lane = lax.broadcasted_iota(jnp.int32, (PEER_SEL, PEER_TOK), 1)
    sub = lax.broadcasted_iota(jnp.int32, (SUBLANES, LANES), 0)
    masks = (sub < 4, (sub % 4) < 2, (sub % 2) == 0)

    def fold(t, part_ref):
        e_row = e_ref.at[t]
        x = x_ref[t]
        x_swapped = pltpu.roll(x, 4, axis=0)
        for grp in range(PEER_SEL // SUBLANES):
            rows = [_expert_row(tbl_ref, e_row[grp * SUBLANES + j]) for j in range(SUBLANES)]
            part_ref[grp * SUBLANES:(grp + 1) * SUBLANES, :] = _fold_rows(rows, x, x_swapped, masks)

    def finish(t, part_ref):
        s = jnp.sum(part_ref[...], axis=1, keepdims=True)
        s_ref[...] = jnp.where(lane == t, s, s_ref[...])

    s_ref[...] = jnp.zeros(s_ref.shape, jnp.float32)
    part_b[...] = jnp.zeros(part_b.shape, jnp.float32)

    def token_pair(i, carry):
        t = 2 * i
        fold(t, part_a)
        finish(t - 1, part_b)
        fold(t + 1, part_b)
        finish(t, part_a)
        return carry

    lax.fori_loop(0, PEER_TOK // 2, token_pair, 0)
    finish(PEER_TOK - 1, part_b)
    gates = gate_ref[...].reshape(PEER_SEL, PEER_TOK)
    w_ref[...] = gates * _gelu_tanh(s_ref[...])


def _peer_u(e_sm, x3, gates, tbl):
    t = x3.shape[0]
    return pl.pallas_call(
        _peer_u_kernel,
        grid=(t // PEER_TOK,),
        in_specs=[
            pl.BlockSpec((PEER_TOK, PEER_SEL), lambda i: (i, 0), memory_space=pltpu.SMEM),
            pl.BlockSpec((PEER_TOK, SUBLANES, LANES), lambda i: (i, 0, 0)),
            pl.BlockSpec((PEER_HEADS, PEER_TOPK, PEER_TOK), lambda i: (0, 0, i)),
            pl.BlockSpec(tbl.shape, lambda i: (0, 0, 0), pipeline_mode=pl.Buffered(1)),
        ],
        out_specs=pl.BlockSpec((PEER_SEL, PEER_TOK), lambda i: (0, i)),
        out_shape=jax.ShapeDtypeStruct((PEER_SEL, t), jnp.float32),
        scratch_shapes=[
            pltpu.VMEM((PEER_SEL, PEER_TOK), jnp.float32),
            pltpu.VMEM((PEER_SEL, LANES), jnp.float32),
            pltpu.VMEM((PEER_SEL, LANES), jnp.float32),
        ],
        compiler_params=_cparams(("arbitrary",)),
        name="peer_u",
    )(e_sm, x3, gates, tbl)


def _peer_v_kernel(e_ref, wt_ref, x1_ref, fg_ref, tbl_ref, y_ref):
    fg = fg_ref[...]
    d_model = SUBLANES * LANES
    lane = lax.broadcasted_iota(jnp.int32, (PEER_SEL, PEER_TOK), 1)

    def spread(t):
        col = jnp.sum(jnp.where(lane == t, wt_ref[...], 0.0), axis=1, keepdims=True)
        return jnp.broadcast_to(col, (PEER_SEL, LANES))

    def token(t, wb):
        wb_next = spread(jnp.minimum(t + 1, PEER_TOK - 1))
        e_row = e_ref.at[t]
        acc = x1_ref[t]
        for grp in range(PEER_SEL // SUBLANES):
            terms = []
            for j in range(SUBLANES):
                k = grp * SUBLANES + j
                terms.append(wb[k:k + 1, :] * _expert_row(tbl_ref, e_row[k]))
            while len(terms) > 1:
                terms = [terms[i] + terms[i + 1] for i in range(0, len(terms), 2)]
            acc = acc + terms[0]
        y_ref[t] = acc
        return wb_next

    lax.fori_loop(0, PEER_TOK, token, spread(0))
    x2 = y_ref[...]
    ms = jnp.sum(x2 * x2, axis=(1, 2), keepdims=True) * (1.0 / d_model)
    y_ref[...] = x2 * lax.rsqrt(ms + EPS) * fg


def _peer_v(e_sm, wt, x13, fg, tbl):
    t = x13.shape[0]
    return pl.pallas_call(
        _peer_v_kernel,
        grid=(t // PEER_TOK,),
        in_specs=[
            pl.BlockSpec((PEER_TOK, PEER_SEL), lambda i: (i, 0), memory_space=pltpu.SMEM),
            pl.BlockSpec((PEER_SEL, PEER_TOK), lambda i: (0, i)),
            pl.BlockSpec((PEER_TOK, SUBLANES, LANES), lambda i: (i, 0, 0)),
            pl.BlockSpec((SUBLANES, LANES), lambda i: (0, 0)),
            pl.BlockSpec(tbl.shape, lambda i: (0, 0, 0), pipeline_mode=pl.Buffered(1)),
        ],
        out_specs=pl.BlockSpec((PEER_TOK, SUBLANES, LANES), lambda i: (i, 0, 0)),
        out_shape=jax.ShapeDtypeStruct((t, SUBLANES, LANES), jnp.float32),
        compiler_params=_cparams(("arbitrary",)),
        name="peer_v",
    )(e_sm, wt, x13, fg, tbl)


def _pack_table(tbl):
    n, d = tbl.shape
    assert d == ROW_TILE * LANES
    return tbl.astype(jnp.bfloat16).reshape(n, ROW_TILE, LANES)


def _rope_tables(seq):
    pos = jnp.arange(seq)
    row = (pos // GRID_W).astype(jnp.float32)
    col = (pos % GRID_W).astype(jnp.float32)
    n_pairs = ROPE_AXIS_DIM // 2
    inv_freq = ROPE_THETA ** (-jnp.arange(n_pairs, dtype=jnp.float32) / n_pairs)
    ang_r = row[:, None] * inv_freq[None, :]
    ang_c = col[:, None] * inv_freq[None, :]
    cos = jnp.concatenate([jnp.cos(ang_r), jnp.cos(ang_r), jnp.cos(ang_c), jnp.cos(ang_c)], axis=1)
    sin = jnp.concatenate([-jnp.sin(ang_r), jnp.sin(ang_r), -jnp.sin(ang_c), jnp.sin(ang_c)], axis=1)
    reps = LANES // HEAD_DIM
    return jnp.tile(cos, (1, reps)), jnp.tile(sin, (1, reps))


def kernel(x, norm1_g, w_in, q_norm_g, k_norm_g, conv_dw, conv_b, conv_ln_g, conv_ln_b, w_out,
           norm2_g, peer_wq, peer_keys, peer_u, peer_v, final_g):
    b, s, d = x.shape
    t = b * s
    assert d == SUBLANES * LANES and s % GRID_W == 0 and N_KV_HEADS == 2
    tm = min(512, s)
    tq = min(512, s)
    tk = min(512, s)
    tr = min(256, s)
    assert s % tm == 0 and t % PEER_TOK == 0

    f32 = jnp.float32
    bf16 = jnp.bfloat16
    x2 = x.reshape(t, d)
    cos_t, sin_t = _rope_tables(s)
    head_id = jnp.arange(ATTN_WIDTH) // HEAD_DIM
    ones_bd = (head_id[:, None] == head_id[None, :]).astype(bf16)

    qt, k, vt, hg = _in_proj(
        x2, norm1_g.reshape(1, d), w_in.astype(bf16),
        jnp.tile(q_norm_g, N_HEADS).reshape(1, ATTN_WIDTH),
        jnp.tile(k_norm_g, N_KV_HEADS).reshape(1, KV_WIDTH),
        cos_t, sin_t, ones_bd, b, s, tm)

    attn = _attention(qt, k, vt, b, s, tq, tk)

    cw = jnp.concatenate([conv_dw.reshape(CONV_W, CONV_CH), jnp.zeros((1, CONV_CH), f32)], axis=0)
    x1, hn = _mix_out(attn, hg, cw, conv_b.reshape(1, CONV_CH), conv_ln_g.reshape(1, CONV_CH),
                      conv_ln_b.reshape(1, CONV_CH), w_out.astype(bf16), x2,
                      norm2_g.reshape(1, d), s, tm)

    e, gates = _route(hn, peer_wq.T.astype(bf16), peer_keys.astype(bf16), tr)

    e_sm = e.reshape(PEER_SEL, t).T
    w = _peer_u(e_sm, hn.reshape(t, SUBLANES, LANES), gates, _pack_table(peer_u))
    y = _peer_v(e_sm, w, x1.reshape(t, SUBLANES, LANES), final_g.reshape(SUBLANES, LANES),
                _pack_table(peer_v))
    return y.reshape(b, s, d)
```

```python
import functools

import jax
import jax.numpy as jnp
import numpy as np
from jax import lax
from jax.experimental import pallas as pl
from jax.experimental.pallas import tpu as pltpu

HEAD_DIM = 64
N_HEADS = 8
N_KV_HEADS = 2
GROUP = N_HEADS // N_KV_HEADS
ATTN_WIDTH = N_HEADS * HEAD_DIM
KV_WIDTH = N_KV_HEADS * HEAD_DIM
CONV_CH = 512
CONV_W = 31
CONV_PAD = (CONV_W - 1) // 2
GRID_W = 64
ROPE_THETA = 10000.0
ROPE_AXIS_DIM = HEAD_DIM // 2
N_KEYS = 128
PEER_HEADS = 8
PEER_DQ = 256
PEER_TOPK = 16
PEER_SEL = PEER_HEADS * PEER_TOPK
EPS = 1e-6

LANES = 128
SUBLANES = 8
HALO = 16
BF16_ROWS = 16
V_EXT = HEAD_DIM + BF16_ROWS
LOG2_E = 1.4426950408889634
VMEM_LIMIT = 56 * 1024 * 1024

ROW_TILE = 8
ROW_WORDS = ROW_TILE // 2

_NEG_INF = float("-inf")


def _cparams(sem, vmem=VMEM_LIMIT):
    return pltpu.CompilerParams(dimension_semantics=sem, vmem_limit_bytes=vmem)


def _group_mean_sq(v, ones_blockdiag):
    sq = v * v
    hi = sq.astype(jnp.bfloat16)
    lo = (sq - hi.astype(jnp.float32)).astype(jnp.bfloat16)
    tot = (jnp.dot(hi, ones_blockdiag, preferred_element_type=jnp.float32)
           + jnp.dot(lo, ones_blockdiag, preferred_element_type=jnp.float32))
    return tot * (1.0 / HEAD_DIM)


def _rope(v, cos, sin_signed, lane_in_pair_lo):
    width = v.shape[-1]
    up = pltpu.roll(v, width - ROPE_AXIS_DIM // 2, axis=1)
    down = pltpu.roll(v, ROPE_AXIS_DIM // 2, axis=1)
    partner = jnp.where(lane_in_pair_lo, up, down)
    return v * cos + partner * sin_signed


def _in_proj_kernel(x_ref, g1_ref, w_ref, qg_ref, kg_ref, cos_ref, sin_ref, ones_ref,
                    qt_ref, k_ref, vt_ref, hg_ref):
    x = x_ref[...]
    ms = jnp.mean(x * x, axis=-1, keepdims=True)
    h = (x * lax.rsqrt(ms + EPS) * g1_ref[...]).astype(jnp.bfloat16)
    p = jnp.dot(h, w_ref[...], preferred_element_type=jnp.float32)
    o1 = ATTN_WIDTH
    o2 = o1 + KV_WIDTH
    o3 = o2 + KV_WIDTH
    o4 = o3 + CONV_CH
    q = p[:, :o1]
    k = p[:, o1:o2]
    v = p[:, o2:o3]
    a = p[:, o3:o4]
    gate = p[:, o4:]

    ones_bd = ones_ref[...]
    qn = q * lax.rsqrt(_group_mean_sq(q, ones_bd) + EPS) * qg_ref[...]
    kn = k * lax.rsqrt(_group_mean_sq(k, ones_bd[:KV_WIDTH, :KV_WIDTH]) + EPS) * kg_ref[...]

    cos2 = cos_ref[...]
    sin2 = sin_ref[...]
    cos_q = jnp.concatenate([cos2] * (ATTN_WIDTH // LANES), axis=1)
    sin_q = jnp.concatenate([sin2] * (ATTN_WIDTH // LANES), axis=1)
    lane_q = lax.broadcasted_iota(jnp.int32, q.shape, 1)
    lane_k = lax.broadcasted_iota(jnp.int32, k.shape, 1)
    half = ROPE_AXIS_DIM // 2
    qr = _rope(qn, cos_q, sin_q, (lane_q % ROPE_AXIS_DIM) < half) * (HEAD_DIM ** -0.5 * LOG2_E)
    kr = _rope(kn, cos2, sin2, (lane_k % ROPE_AXIS_DIM) < half)

    qt = qr.T.astype(qt_ref.dtype)
    zeros = jnp.zeros((HEAD_DIM, qt.shape[1]), qt_ref.dtype)
    for hh in range(N_HEADS):
        g = hh // GROUP
        piece = qt[hh * HEAD_DIM:(hh + 1) * HEAD_DIM, :]
        qt_ref[0, hh, g * HEAD_DIM:(g + 1) * HEAD_DIM, :] = piece
        qt_ref[0, hh, (1 - g) * HEAD_DIM:(2 - g) * HEAD_DIM, :] = zeros
    k_ref[...] = kr.astype(k_ref.dtype)
    vt = v.T.astype(vt_ref.dtype)
    pad_row = lax.broadcasted_iota(jnp.int32, (V_EXT - HEAD_DIM, vt.shape[1]), 0)
    ones_pad = jnp.where(pad_row == 0, 1.0, 0.0).astype(vt_ref.dtype)
    for g in range(N_KV_HEADS):
        vt_ref[0, g, 0:HEAD_DIM, :] = vt[g * HEAD_DIM:(g + 1) * HEAD_DIM, :]
        vt_ref[0, g, HEAD_DIM:V_EXT, :] = ones_pad
    hg_ref[...] = a * (1.0 / (1.0 + jnp.exp(-gate)))


def _in_proj(x2, g1, w_in, qg, kg, cos_t, sin_t, ones_bd, batch, seq, tm):
    t, d = x2.shape
    n_s = seq // tm
    in_width = w_in.shape[1]
    return pl.pallas_call(
        _in_proj_kernel,
        grid=(t // tm,),
        in_specs=[
            pl.BlockSpec((tm, d), lambda i: (i, 0)),
            pl.BlockSpec((1, d), lambda i: (0, 0)),
            pl.BlockSpec((d, in_width), lambda i: (0, 0)),
            pl.BlockSpec((1, ATTN_WIDTH), lambda i: (0, 0)),
            pl.BlockSpec((1, KV_WIDTH), lambda i: (0, 0)),
            pl.BlockSpec((tm, LANES), lambda i: (i % n_s, 0)),
            pl.BlockSpec((tm, LANES), lambda i: (i % n_s, 0)),
            pl.BlockSpec((ATTN_WIDTH, ATTN_WIDTH), lambda i: (0, 0)),
        ],
        out_specs=[
            pl.BlockSpec((1, N_HEADS, KV_WIDTH, tm), lambda i: (i // n_s, 0, 0, i % n_s)),
            pl.BlockSpec((tm, KV_WIDTH), lambda i: (i, 0)),
            pl.BlockSpec((1, N_KV_HEADS, V_EXT, tm), lambda i: (i // n_s, 0, 0, i % n_s)),
            pl.BlockSpec((tm, CONV_CH), lambda i: (i, 0)),
        ],
        out_shape=[
            jax.ShapeDtypeStruct((batch, N_HEADS, KV_WIDTH, seq), jnp.bfloat16),
            jax.ShapeDtypeStruct((t, KV_WIDTH), jnp.bfloat16),
            jax.ShapeDtypeStruct((batch, N_KV_HEADS, V_EXT, seq), jnp.bfloat16),
            jax.ShapeDtypeStruct((t, CONV_CH), jnp.float32),
        ],
        compiler_params=_cparams(("parallel",)),
        name="in_proj",
    )(x2, g1, w_in, qg, kg, cos_t, sin_t, ones_bd)


SCORES_AHEAD = 2


def _attn_kernel(qt_ref, k_ref, vt_ref, o_ref, m_ref, acc_ref):
    ki = pl.program_id(2)

    @pl.when(ki == 0)
    def _():
        m_ref[...] = jnp.full(m_ref.shape, _NEG_INF, jnp.float32)
        acc_ref[...] = jnp.zeros(acc_ref.shape, jnp.float32)

    k = k_ref[...]

    def scores(h):
        return jnp.dot(k, qt_ref[0, h], preferred_element_type=jnp.float32)

    ahead = [scores(h) for h in range(SCORES_AHEAD)]
    for h in range(N_HEADS):
        g = h // GROUP
        s = ahead.pop(0)
        if h + SCORES_AHEAD < N_HEADS:
            ahead.append(scores(h + SCORES_AHEAD))
        m_prev = m_ref[h:h + 1, :]
        m_new = jnp.maximum(m_prev, jnp.max(s, axis=0, keepdims=True))
        alpha = jnp.exp2(m_prev - m_new)
        p = jnp.exp2(s - m_new).astype(jnp.bfloat16)
        pv = jnp.dot(vt_ref[0, g], p, preferred_element_type=jnp.float32)
        acc_ref[h] = alpha * acc_ref[h] + pv
        m_ref[h:h + 1, :] = m_new

    @pl.when(ki == pl.num_programs(2) - 1)
    def _():
        outs = []
        for h in range(N_HEADS):
            acc = acc_ref[h]
            outs.append(acc[:HEAD_DIM, :] / acc[HEAD_DIM:HEAD_DIM + 1, :])
        o_ref[...] = jnp.concatenate(outs, axis=0).T.astype(o_ref.dtype)


def _attention(qt, k, vt, batch, seq, tq, tk):
    t = k.shape[0]
    nq = seq // tq
    nk = seq // tk
    return pl.pallas_call(
        _attn_kernel,
        grid=(batch, nq, nk),
        in_specs=[
            pl.BlockSpec((1, N_HEADS, KV_WIDTH, tq), lambda b, qi, ki: (b, 0, 0, qi)),
            pl.BlockSpec((tk, KV_WIDTH), lambda b, qi, ki: (b * nk + ki, 0)),
            pl.BlockSpec((1, N_KV_HEADS, V_EXT, tk), lambda b, qi, ki: (b, 0, 0, ki)),
        ],
        out_specs=pl.BlockSpec((tq, ATTN_WIDTH), lambda b, qi, ki: (b * nq + qi, 0)),
        out_shape=jax.ShapeDtypeStruct((t, ATTN_WIDTH), jnp.bfloat16),
        scratch_shapes=[
            pltpu.VMEM((N_HEADS, tq), jnp.float32),
            pltpu.VMEM((N_HEADS, V_EXT, tq), jnp.float32),
        ],
        compiler_params=_cparams(("parallel", "parallel", "arbitrary")),
        name="attention",
    )(qt, k, vt)


CONV_ROWS = 32


def _mix_out_kernel(attn_ref, hg_ref, prev_ref, next_ref, cw_ref, cb_ref, lg_ref, lb_ref,
                    wo_ref, x_ref, g2_ref, x1_ref, hn_ref, ext_ref, conv_ref, *, n_s):
    tm = hg_ref.shape[0]
    si = pl.program_id(0) % n_s
    keep_prev = (si > 0).astype(jnp.float32)
    keep_next = (si < n_s - 1).astype(jnp.float32)
    ext_ref[0:HALO, :] = prev_ref[...] * keep_prev
    ext_ref[HALO:HALO + tm, :] = hg_ref[...]
    ext_ref[HALO + tm:HALO + tm + HALO, :] = next_ref[...] * keep_next

    cw = cw_ref[...]
    cb = cb_ref[...]
    lg = lg_ref[...]
    lb = lb_ref[...]

    def conv_step(c, carry):
        start = pl.multiple_of(c * CONV_ROWS, CONV_ROWS)
        win = ext_ref[pl.ds(start, CONV_ROWS + 2 * HALO), :]
        acc = jnp.zeros((CONV_ROWS, CONV_CH), jnp.float32)
        for j in range(CONV_W):
            off = HALO - CONV_PAD + j
            acc = acc + win[off:off + CONV_ROWS, :] * cw[j:j + 1, :]
        acc = acc + cb
        mu = jnp.mean(acc, axis=-1, keepdims=True)
        cen = acc - mu
        var = jnp.mean(cen * cen, axis=-1, keepdims=True)
        y = cen * lax.rsqrt(var + EPS) * lg + lb
        y = y * (1.0 / (1.0 + jnp.exp(-y)))
        conv_ref[pl.ds(start, CONV_ROWS), :] = y.astype(conv_ref.dtype)
        return carry

    lax.fori_loop(0, tm // CONV_ROWS, conv_step, 0)

    mixed = (jnp.dot(attn_ref[...], wo_ref[0:ATTN_WIDTH, :], preferred_element_type=jnp.float32)
             + jnp.dot(conv_ref[...], wo_ref[ATTN_WIDTH:, :], preferred_element_type=jnp.float32))
    x1 = x_ref[...] + mixed
    x1_ref[...] = x1
    ms = jnp.mean(x1 * x1, axis=-1, keepdims=True)
    hn_ref[...] = x1 * lax.rsqrt(ms + EPS) * g2_ref[...]


def _mix_out(attn, hg, cw, cb, lg, lb, w_out, x2, g2, seq, tm):
    t, d = x2.shape
    n_s = seq // tm
    hb = tm // HALO
    last_halo = t // HALO - 1
    return pl.pallas_call(
        functools.partial(_mix_out_kernel, n_s=n_s),
        grid=(t // tm,),
        in_specs=[
            pl.BlockSpec((tm, ATTN_WIDTH), lambda i: (i, 0)),
            pl.BlockSpec((tm, CONV_CH), lambda i: (i, 0)),
            pl.BlockSpec((HALO, CONV_CH), lambda i: (jnp.maximum(i * hb - 1, 0), 0)),
            pl.BlockSpec((HALO, CONV_CH), lambda i: (jnp.minimum((i + 1) * hb, last_halo), 0)),
            pl.BlockSpec((CONV_W + 1, CONV_CH), lambda i: (0, 0)),
            pl.BlockSpec((1, CONV_CH), lambda i: (0, 0)),
            pl.BlockSpec((1, CONV_CH), lambda i: (0, 0)),
            pl.BlockSpec((1, CONV_CH), lambda i: (0, 0)),
            pl.BlockSpec((d, d), lambda i: (0, 0)),
            pl.BlockSpec((tm, d), lambda i: (i, 0)),
            pl.BlockSpec((1, d), lambda i: (0, 0)),
        ],
        out_specs=[
            pl.BlockSpec((tm, d), lambda i: (i, 0)),
            pl.BlockSpec((tm, d), lambda i: (i, 0)),
        ],
        out_shape=[
            jax.ShapeDtypeStruct((t, d), jnp.float32),
            jax.ShapeDtypeStruct((t, d), jnp.float32),
        ],
        scratch_shapes=[
            pltpu.VMEM((tm + 2 * HALO, CONV_CH), jnp.float32),
            pltpu.VMEM((tm, CONV_CH), jnp.bfloat16),
        ],
        compiler_params=_cparams(("parallel",)),
        name="mix_out",
    )(attn, hg, hg, hg, cw, cb, lg, lb, w_out, x2, g2)


def _top_rows(s, order, payload, count):
    big = jnp.int32(2 ** 30)
    vals = []
    picks = []
    for _ in range(count):
        m = jnp.max(s, axis=0, keepdims=True)
        first = jnp.min(jnp.where(s == m, order, big), axis=0, keepdims=True)
        chosen = order == first
        if payload is order:
            pick = first
        else:
            pick = jnp.max(jnp.where(chosen, payload, -1), axis=0, keepdims=True)
        vals.append(m)
        picks.append(pick)
        s = jnp.where(chosen, _NEG_INF, s)
    return jnp.concatenate(vals, axis=0), jnp.concatenate(picks, axis=0)


def _candidate_blocks():
    blocks = [(0, 0, 16)]
    for a in range(1, 8):
        blocks.append((a, 0, 8))
    return blocks


def _route_kernel(hn_ref, wqt_ref, keys_ref, e_ref, g_ref):
    hn = hn_ref[...].astype(jnp.bfloat16)
    qpt = lax.dot_general(wqt_ref[...], hn, (((1,), (1,)), ((), ())),
                          preferred_element_type=jnp.float32)
    qpt = qpt.astype(jnp.bfloat16)
    tm = hn.shape[0]
    half = PEER_DQ // 2
    k0 = keys_ref[0]
    k1 = keys_ref[1]
    row_iota = lax.broadcasted_iota(jnp.int32, (N_KEYS, LANES), 0)

    blocks = _candidate_blocks()
    flat_parts = []
    for a, b0, nb in blocks:
        flat_parts.append(a * PEER_TOPK + b0 + lax.broadcasted_iota(jnp.int32, (nb, LANES), 0))
    tail_flat = (8 + lax.broadcasted_iota(jnp.int32, (8, LANES), 0)) * PEER_TOPK
    flat = jnp.concatenate(flat_parts + [tail_flat], axis=0)

    for h in range(PEER_HEADS):
        q1 = qpt[h * PEER_DQ:h * PEER_DQ + half, :]
        q2 = qpt[h * PEER_DQ + half:(h + 1) * PEER_DQ, :]
        s1 = jnp.dot(k0, q1, preferred_element_type=jnp.float32)
        s2 = jnp.dot(k1, q2, preferred_element_type=jnp.float32)
        for c in range(tm // LANES):
            sl = slice(c * LANES, (c + 1) * LANES)
            v1, i1 = _top_rows(s1[:, sl], row_iota, row_iota, PEER_TOPK)
            v2, i2 = _top_rows(s2[:, sl], row_iota, row_iota, PEER_TOPK)
            cs = []
            ce = []
            for a, b0, nb in blocks:
                cs.append(v1[a:a + 1, :] + v2[b0:b0 + nb, :])
                ce.append(i1[a:a + 1, :] * N_KEYS + i2[b0:b0 + nb, :])
            cs.append(v1[8:16, :] + v2[0:1, :])
            ce.append(i1[8:16, :] * N_KEYS + i2[0:1, :])
            cand_s = jnp.concatenate(cs, axis=0)
            cand_e = jnp.concatenate(ce, axis=0)
            top_s, top_e = _top_rows(cand_s, flat, cand_e, PEER_TOPK)
            ex = jnp.exp(top_s - top_s[0:1, :])
            gates = ex / jnp.sum(ex, axis=0, keepdims=True)
            off = top_e * ROW_WORDS
            half_k = PEER_TOPK // 2
            e_ref[h, :, sl] = (off[half_k:, :] << 16) | off[:half_k, :]
            g_ref[h, :, sl] = gates


def _route(hn, wqt, keys, tm):
    t, d = hn.shape
    return pl.pallas_call(
        _route_kernel,
        grid=(t // tm,),
        in_specs=[
            pl.BlockSpec((tm, d), lambda i: (i, 0)),
            pl.BlockSpec(wqt.shape, lambda i: (0, 0)),
            pl.BlockSpec(keys.shape, lambda i: (0, 0, 0)),
        ],
        out_specs=[
            pl.BlockSpec((PEER_HEADS, PEER_TOPK // 2, tm), lambda i: (0, 0, i)),
            pl.BlockSpec((PEER_HEADS, PEER_TOPK, tm), lambda i: (0, 0, i)),
        ],
        out_shape=[
            jax.ShapeDtypeStruct((PEER_HEADS, PEER_TOPK // 2, t), jnp.int32),
            jax.ShapeDtypeStruct((PEER_HEADS, PEER_TOPK, t), jnp.float32),
        ],
        compiler_params=_cparams(("parallel",)),
        name="route",
    )(hn, wqt, keys)


PEER_TOK = 128


def _expert_row(tbl_ref, off):
    words = tbl_ref[pl.ds(pl.multiple_of(off, ROW_WORDS), ROW_WORDS), :]
    return pltpu.bitcast(words, jnp.bfloat16).astype(jnp.float32)


def _expert_offsets(idx_row, head):
    lo, hi = [], []
    for j in range(PEER_TOPK // 2):
        word = idx_row[head * (PEER_TOPK // 2) + j]
        lo.append(word & 0xFFFF)
        hi.append(lax.shift_right_logical(word, 16))
    return lo + hi


def _gelu_tanh(x):
    c = np.float32(np.sqrt(2.0 / np.pi))
    return 0.5 * x * (1.0 + jnp.tanh(c * (x + 0.044715 * (x * x * x))))


def _fold_rows(rows, x, x_swapped, masks):
    lo4, mod4_lo2, even = masks
    a, e, c, g, b, f, d, h = rows

    def level1(p, q):
        t1 = jnp.where(lo4, p, q)
        t2 = pltpu.roll(jnp.where(lo4, q, p), 4, axis=0)
        return t1 * x + t2 * x_swapped

    def level2(u, v):
        uu = u + pltpu.roll(u, 6, axis=0)
        vv = v + pltpu.roll(v, 2, axis=0)
        return jnp.where(mod4_lo2, uu, vv)

    def level3(u, v):
        uu = u + pltpu.roll(u, 7, axis=0)
        vv = v + pltpu.roll(v, 1, axis=0)
        return jnp.where(even, uu, vv)

    return level3(level2(level1(a, b), level1(c, d)), level2(level1(e, f), level1(g, h)))


def _peer_u_kernel(e_ref, x_ref, gate_ref, tbl_ref, w_ref, s_ref, part_a, part_b):
    lane = lax.broadcasted_iota(jnp.int32, (PEER_SEL, PEER_TOK), 1)
    sub = lax.broadcasted_iota(jnp.int32, (SUBLANES, LANES), 0)
    masks = (sub < 4, (sub % 4) < 2, (sub % 2) == 0)

    def fold(t, part_ref):
        e_row = e_ref.at[t]
        x = x_ref[t]
        x_swapped = pltpu.roll(x, 4, axis=0)
        for head in range(PEER_HEADS):
            offs = _expert_offsets(e_row, head)
            for half in range(PEER_TOPK // SUBLANES):
                grp = head * (PEER_TOPK // SUBLANES) + half
                rows = [_expert_row(tbl_ref, offs[half * SUBLANES + j]) for j in range(SUBLANES)]
                part_ref[grp * SUBLANES:(grp + 1) * SUBLANES, :] = _fold_rows(rows, x, x_swapped, masks)

    def finish(t, part_ref):
        s = jnp.sum(part_ref[...], axis=1, keepdims=True)
        s_ref[...] = jnp.where(lane == t, s, s_ref[...])

    s_ref[...] = jnp.zeros(s_ref.shape, jnp.float32)
    part_b[...] = jnp.zeros(part_b.shape, jnp.float32)

    def token_pair(i, carry):
        t = 2 * i
        fold(t, part_a)
        finish(t - 1, part_b)
        fold(t + 1, part_b)
        finish(t, part_a)
        return carry

    lax.fori_loop(0, PEER_TOK // 2, token_pair, 0)
    finish(PEER_TOK - 1, part_b)
    gates = gate_ref[...].reshape(PEER_SEL, PEER_TOK)
    w_ref[...] = gates * _gelu_tanh(s_ref[...])


def _peer_u(e_sm, x3, gates, tbl):
    t = x3.shape[0]
    return pl.pallas_call(
        _peer_u_kernel,
        grid=(t // PEER_TOK,),
        in_specs=[
            pl.BlockSpec((PEER_TOK, PEER_SEL // 2), lambda i: (i, 0), memory_space=pltpu.SMEM),
            pl.BlockSpec((PEER_TOK, SUBLANES, LANES), lambda i: (i, 0, 0)),
            pl.BlockSpec((PEER_HEADS, PEER_TOPK, PEER_TOK), lambda i: (0, 0, i)),
            pl.BlockSpec(tbl.shape, lambda i: (0, 0), pipeline_mode=pl.Buffered(1)),
        ],
        out_specs=pl.BlockSpec((PEER_SEL, PEER_TOK), lambda i: (0, i)),
        out_shape=jax.ShapeDtypeStruct((PEER_SEL, t), jnp.float32),
        scratch_shapes=[
            pltpu.VMEM((PEER_SEL, PEER_TOK), jnp.float32),
            pltpu.VMEM((PEER_SEL, LANES), jnp.float32),
            pltpu.VMEM((PEER_SEL, LANES), jnp.float32),
        ],
        compiler_params=_cparams(("arbitrary",)),
        name="peer_u",
    )(e_sm, x3, gates, tbl)


def _peer_v_kernel(e_ref, wt_ref, x1_ref, fg_ref, tbl_ref, y_ref):
    fg = fg_ref[...]
    d_model = SUBLANES * LANES
    lane = lax.broadcasted_iota(jnp.int32, (PEER_SEL, PEER_TOK), 1)

    def spread(t):
        col = jnp.sum(jnp.where(lane == t, wt_ref[...], 0.0), axis=1, keepdims=True)
        return jnp.broadcast_to(col, (PEER_SEL, LANES))

    def token(t, wb):
        wb_next = spread(jnp.minimum(t + 1, PEER_TOK - 1))
        e_row = e_ref.at[t]
        acc = x1_ref[t]
        for head in range(PEER_HEADS):
            offs = _expert_offsets(e_row, head)
            for half in range(PEER_TOPK // SUBLANES):
                terms = []
                for j in range(SUBLANES):
                    slot = half * SUBLANES + j
                    k = head * PEER_TOPK + slot
                    terms.append(wb[k:k + 1, :] * _expert_row(tbl_ref, offs[slot]))
                while len(terms) > 1:
                    terms = [terms[i] + terms[i + 1] for i in range(0, len(terms), 2)]
                acc = acc + terms[0]
        y_ref[t] = acc
        return wb_next

    lax.fori_loop(0, PEER_TOK, token, spread(0))
    x2 = y_ref[...]
    ms = jnp.sum(x2 * x2, axis=(1, 2), keepdims=True) * (1.0 / d_model)
    y_ref[...] = x2 * lax.rsqrt(ms + EPS) * fg


def _peer_v(e_sm, wt, x13, fg, tbl):
    t = x13.shape[0]
    return pl.pallas_call(
        _peer_v_kernel,
        grid=(t // PEER_TOK,),
        in_specs=[
            pl.BlockSpec((PEER_TOK, PEER_SEL // 2), lambda i: (i, 0), memory_space=pltpu.SMEM),
            pl.BlockSpec((PEER_SEL, PEER_TOK), lambda i: (0, i)),
            pl.BlockSpec((PEER_TOK, SUBLANES, LANES), lambda i: (i, 0, 0)),
            pl.BlockSpec((SUBLANES, LANES), lambda i: (0, 0)),
            pl.BlockSpec(tbl.shape, lambda i: (0, 0), pipeline_mode=pl.Buffered(1)),
        ],
        out_specs=pl.BlockSpec((PEER_TOK, SUBLANES, LANES), lambda i: (i, 0, 0)),
        out_shape=jax.ShapeDtypeStruct((t, SUBLANES, LANES), jnp.float32),
        compiler_params=_cparams(("arbitrary",)),
        name="peer_v",
    )(e_sm, wt, x13, fg, tbl)


def _pack_table(tbl):
    n, d = tbl.shape
    assert d == ROW_TILE * LANES
    pairs = tbl.astype(jnp.bfloat16).reshape(n, ROW_WORDS, 2, LANES).transpose(0, 1, 3, 2)
    return lax.bitcast_convert_type(pairs, jnp.int32).reshape(n * ROW_WORDS, LANES)


def _rope_tables(seq):
    pos = jnp.arange(seq)
    row = (pos // GRID_W).astype(jnp.float32)
    col = (pos % GRID_W).astype(jnp.float32)
    n_pairs = ROPE_AXIS_DIM // 2
    inv_freq = ROPE_THETA ** (-jnp.arange(n_pairs, dtype=jnp.float32) / n_pairs)
    ang_r = row[:, None] * inv_freq[None, :]
    ang_c = col[:, None] * inv_freq[None, :]
    cos = jnp.concatenate([jnp.cos(ang_r), jnp.cos(ang_r), jnp.cos(ang_c), jnp.cos(ang_c)], axis=1)
    sin = jnp.concatenate([-jnp.sin(ang_r), jnp.sin(ang_r), -jnp.sin(ang_c), jnp.sin(ang_c)], axis=1)
    reps = LANES // HEAD_DIM
    return jnp.tile(cos, (1, reps)), jnp.tile(sin, (1, reps))


def kernel(x, norm1_g, w_in, q_norm_g, k_norm_g, conv_dw, conv_b, conv_ln_g, conv_ln_b, w_out,
           norm2_g, peer_wq, peer_keys, peer_u, peer_v, final_g):
    b, s, d = x.shape
    t = b * s
    assert d == SUBLANES * LANES and s % GRID_W == 0 and N_KV_HEADS == 2
    tm = min(512, s)
    tq = min(512, s)
    tk = min(512, s)
    tr = min(256, s)
    assert s % tm == 0 and t % PEER_TOK == 0

    f32 = jnp.float32
    bf16 = jnp.bfloat16
    x2 = x.reshape(t, d)
    cos_t, sin_t = _rope_tables(s)
    head_id = jnp.arange(ATTN_WIDTH) // HEAD_DIM
    ones_bd = (head_id[:, None] == head_id[None, :]).astype(bf16)

    qt, k, vt, hg = _in_proj(
        x2, norm1_g.reshape(1, d), w_in.astype(bf16),
        jnp.tile(q_norm_g, N_HEADS).reshape(1, ATTN_WIDTH),
        jnp.tile(k_norm_g, N_KV_HEADS).reshape(1, KV_WIDTH),
        cos_t, sin_t, ones_bd, b, s, tm)

    attn = _attention(qt, k, vt, b, s, tq, tk)

    cw = jnp.concatenate([conv_dw.reshape(CONV_W, CONV_CH), jnp.zeros((1, CONV_CH), f32)], axis=0)
    x1, hn = _mix_out(attn, hg, cw, conv_b.reshape(1, CONV_CH), conv_ln_g.reshape(1, CONV_CH),
                      conv_ln_b.reshape(1, CONV_CH), w_out.astype(bf16), x2,
                      norm2_g.reshape(1, d), s, tm)

    e, gates = _route(hn, peer_wq.T.astype(bf16), peer_keys.astype(bf16), tr)

    e_sm = e.reshape(PEER_SEL // 2, t).T
    w = _peer_u(e_sm, hn.reshape(t, SUBLANES, LANES), gates, _pack_table(peer_u))
    y = _peer_v(e_sm, w, x1.reshape(t, SUBLANES, LANES), final_g.reshape(SUBLANES, LANES),
                _pack_table(peer_v))
    return y.reshape(b, s, d)
```

```python
import functools

import jax
import jax.numpy as jnp
import numpy as np
from jax import lax
from jax.experimental import pallas as pl
from jax.experimental.pallas import tpu as pltpu

HEAD_DIM = 64
N_HEADS = 8
N_KV_HEADS = 2
GROUP = N_HEADS // N_KV_HEADS
ATTN_WIDTH = N_HEADS * HEAD_DIM
KV_WIDTH = N_KV_HEADS * HEAD_DIM
CONV_CH = 512
CONV_W = 31
CONV_PAD = (CONV_W - 1) // 2
GRID_W = 64
ROPE_THETA = 10000.0
ROPE_AXIS_DIM = HEAD_DIM // 2
N_KEYS = 128
PEER_HEADS = 8
PEER_DQ = 256
PEER_TOPK = 16
PEER_SEL = PEER_HEADS * PEER_TOPK
EPS = 1e-6

LANES = 128
SUBLANES = 8
HALO = 16
BF16_ROWS = 16
V_EXT = HEAD_DIM + BF16_ROWS
LOG2_E = 1.4426950408889634
VMEM_LIMIT = 56 * 1024 * 1024

ROW_TILE = 8
ROW_WORDS = ROW_TILE // 2

_NEG_INF = float("-inf")


def _cparams(sem, vmem=VMEM_LIMIT):
    return pltpu.CompilerParams(dimension_semantics=sem, vmem_limit_bytes=vmem)


def _group_mean_sq(v, ones_blockdiag):
    sq = v * v
    hi = sq.astype(jnp.bfloat16)
    lo = (sq - hi.astype(jnp.float32)).astype(jnp.bfloat16)
    tot = (jnp.dot(hi, ones_blockdiag, preferred_element_type=jnp.float32)
           + jnp.dot(lo, ones_blockdiag, preferred_element_type=jnp.float32))
    return tot * (1.0 / HEAD_DIM)


def _rope(v, cos, sin_signed, lane_in_pair_lo):
    width = v.shape[-1]
    up = pltpu.roll(v, width - ROPE_AXIS_DIM // 2, axis=1)
    down = pltpu.roll(v, ROPE_AXIS_DIM // 2, axis=1)
    partner = jnp.where(lane_in_pair_lo, up, down)
    return v * cos + partner * sin_signed


def _in_proj_kernel(x_ref, g1_ref, w_ref, qg_ref, kg_ref, cos_ref, sin_ref, ones_ref,
                    qt_ref, k_ref, vt_ref, hg_ref):
    x = x_ref[...]
    ms = jnp.mean(x * x, axis=-1, keepdims=True)
    h = (x * lax.rsqrt(ms + EPS) * g1_ref[...]).astype(jnp.bfloat16)
    p = jnp.dot(h, w_ref[...], preferred_element_type=jnp.float32)
    o1 = ATTN_WIDTH
    o2 = o1 + KV_WIDTH
    o3 = o2 + KV_WIDTH
    o4 = o3 + CONV_CH
    q = p[:, :o1]
    k = p[:, o1:o2]
    v = p[:, o2:o3]
    a = p[:, o3:o4]
    gate = p[:, o4:]

    ones_bd = ones_ref[...]
    qn = q * lax.rsqrt(_group_mean_sq(q, ones_bd) + EPS) * qg_ref[...]
    kn = k * lax.rsqrt(_group_mean_sq(k, ones_bd[:KV_WIDTH, :KV_WIDTH]) + EPS) * kg_ref[...]

    cos2 = cos_ref[...]
    sin2 = sin_ref[...]
    cos_q = jnp.concatenate([cos2] * (ATTN_WIDTH // LANES), axis=1)
    sin_q = jnp.concatenate([sin2] * (ATTN_WIDTH // LANES), axis=1)
    lane_q = lax.broadcasted_iota(jnp.int32, q.shape, 1)
    lane_k = lax.broadcasted_iota(jnp.int32, k.shape, 1)
    half = ROPE_AXIS_DIM // 2
    qr = _rope(qn, cos_q, sin_q, (lane_q % ROPE_AXIS_DIM) < half) * (HEAD_DIM ** -0.5 * LOG2_E)
    kr = _rope(kn, cos2, sin2, (lane_k % ROPE_AXIS_DIM) < half)

    qt = qr.T.astype(qt_ref.dtype)
    zeros = jnp.zeros((HEAD_DIM, qt.shape[1]), qt_ref.dtype)
    for hh in range(N_HEADS):
        g = hh // GROUP
        piece = qt[hh * HEAD_DIM:(hh + 1) * HEAD_DIM, :]
        qt_ref[0, hh, g * HEAD_DIM:(g + 1) * HEAD_DIM, :] = piece
        qt_ref[0, hh, (1 - g) * HEAD_DIM:(2 - g) * HEAD_DIM, :] = zeros
    k_ref[...] = kr.astype(k_ref.dtype)
    vt = v.T.astype(vt_ref.dtype)
    pad_row = lax.broadcasted_iota(jnp.int32, (V_EXT - HEAD_DIM, vt.shape[1]), 0)
    ones_pad = jnp.where(pad_row == 0, 1.0, 0.0).astype(vt_ref.dtype)
    for g in range(N_KV_HEADS):
        vt_ref[0, g, 0:HEAD_DIM, :] = vt[g * HEAD_DIM:(g + 1) * HEAD_DIM, :]
        vt_ref[0, g, HEAD_DIM:V_EXT, :] = ones_pad
    hg_ref[...] = a * (1.0 / (1.0 + jnp.exp(-gate)))


def _in_proj(x2, g1, w_in, qg, kg, cos_t, sin_t, ones_bd, batch, seq, tm):
    t, d = x2.shape
    n_s = seq // tm
    in_width = w_in.shape[1]
    return pl.pallas_call(
        _in_proj_kernel,
        grid=(t // tm,),
        in_specs=[
            pl.BlockSpec((tm, d), lambda i: (i, 0)),
            pl.BlockSpec((1, d), lambda i: (0, 0)),
            pl.BlockSpec((d, in_width), lambda i: (0, 0)),
            pl.BlockSpec((1, ATTN_WIDTH), lambda i: (0, 0)),
            pl.BlockSpec((1, KV_WIDTH), lambda i: (0, 0)),
            pl.BlockSpec((tm, LANES), lambda i: (i % n_s, 0)),
            pl.BlockSpec((tm, LANES), lambda i: (i % n_s, 0)),
            pl.BlockSpec((ATTN_WIDTH, ATTN_WIDTH), lambda i: (0, 0)),
        ],
        out_specs=[
            pl.BlockSpec((1, N_HEADS, KV_WIDTH, tm), lambda i: (i // n_s, 0, 0, i % n_s)),
            pl.BlockSpec((tm, KV_WIDTH), lambda i: (i, 0)),
            pl.BlockSpec((1, N_KV_HEADS, V_EXT, tm), lambda i: (i // n_s, 0, 0, i % n_s)),
            pl.BlockSpec((tm, CONV_CH), lambda i: (i, 0)),
        ],
        out_shape=[
            jax.ShapeDtypeStruct((batch, N_HEADS, KV_WIDTH, seq), jnp.bfloat16),
            jax.ShapeDtypeStruct((t, KV_WIDTH), jnp.bfloat16),
            jax.ShapeDtypeStruct((batch, N_KV_HEADS, V_EXT, seq), jnp.bfloat16),
            jax.ShapeDtypeStruct((t, CONV_CH), jnp.float32),
        ],
        compiler_params=_cparams(("parallel",)),
        name="in_proj",
    )(x2, g1, w_in, qg, kg, cos_t, sin_t, ones_bd)


SCORES_AHEAD = 2


def _attn_kernel(qt_ref, k_ref, vt_ref, o_ref, m_ref, acc_ref):
    ki = pl.program_id(2)

    @pl.when(ki == 0)
    def _():
        m_ref[...] = jnp.full(m_ref.shape, _NEG_INF, jnp.float32)
        acc_ref[...] = jnp.zeros(acc_ref.shape, jnp.float32)

    k = k_ref[...]

    def scores(h):
        return jnp.dot(k, qt_ref[0, h], preferred_element_type=jnp.float32)

    ahead = [scores(h) for h in range(SCORES_AHEAD)]
    for h in range(N_HEADS):
        g = h // GROUP
        s = ahead.pop(0)
        if h + SCORES_AHEAD < N_HEADS:
            ahead.append(scores(h + SCORES_AHEAD))
        m_prev = m_ref[h:h + 1, :]
        m_new = jnp.maximum(m_prev, jnp.max(s, axis=0, keepdims=True))
        alpha = jnp.exp2(m_prev - m_new)
        p = jnp.exp2(s - m_new).astype(jnp.bfloat16)
        pv = jnp.dot(vt_ref[0, g], p, preferred_element_type=jnp.float32)
        acc_ref[h] = alpha * acc_ref[h] + pv
        m_ref[h:h + 1, :] = m_new

    @pl.when(ki == pl.num_programs(2) - 1)
    def _():
        outs = []
        for h in range(N_HEADS):
            acc = acc_ref[h]
            outs.append(acc[:HEAD_DIM, :] / acc[HEAD_DIM:HEAD_DIM + 1, :])
        o_ref[...] = jnp.concatenate(outs, axis=0).T.astype(o_ref.dtype)


def _attention(qt, k, vt, batch, seq, tq, tk):
    t = k.shape[0]
    nq = seq // tq
    nk = seq // tk
    return pl.pallas_call(
        _attn_kernel,
        grid=(batch, nq, nk),
        in_specs=[
            pl.BlockSpec((1, N_HEADS, KV_WIDTH, tq), lambda b, qi, ki: (b, 0, 0, qi)),
            pl.BlockSpec((tk, KV_WIDTH), lambda b, qi, ki: (b * nk + ki, 0)),
            pl.BlockSpec((1, N_KV_HEADS, V_EXT, tk), lambda b, qi, ki: (b, 0, 0, ki)),
        ],
        out_specs=pl.BlockSpec((tq, ATTN_WIDTH), lambda b, qi, ki: (b * nq + qi, 0)),
        out_shape=jax.ShapeDtypeStruct((t, ATTN_WIDTH), jnp.bfloat16),
        scratch_shapes=[
            pltpu.VMEM((N_HEADS, tq), jnp.float32),
            pltpu.VMEM((N_HEADS, V_EXT, tq), jnp.float32),
        ],
        compiler_params=_cparams(("parallel", "parallel", "arbitrary")),
        name="attention",
    )(qt, k, vt)


CONV_ROWS = 32


def _mix_out_kernel(attn_ref, hg_ref, prev_ref, next_ref, cw_ref, cb_ref, lg_ref, lb_ref,
                    wo_ref, x_ref, g2_ref, x1_ref, hn_ref, ext_ref, conv_ref, *, n_s):
    tm = hg_ref.shape[0]
    si = pl.program_id(0) % n_s
    keep_prev = (si > 0).astype(jnp.float32)
    keep_next = (si < n_s - 1).astype(jnp.float32)
    ext_ref[0:HALO, :] = prev_ref[...] * keep_prev
    ext_ref[HALO:HALO + tm, :] = hg_ref[...]
    ext_ref[HALO + tm:HALO + tm + HALO, :] = next_ref[...] * keep_next

    cw = cw_ref[...]
    cb = cb_ref[...]
    lg = lg_ref[...]
    lb = lb_ref[...]

    def conv_step(c, carry):
        start = pl.multiple_of(c * CONV_ROWS, CONV_ROWS)
        win = ext_ref[pl.ds(start, CONV_ROWS + 2 * HALO), :]
        acc = jnp.zeros((CONV_ROWS, CONV_CH), jnp.float32)
        for j in range(CONV_W):
            off = HALO - CONV_PAD + j
            acc = acc + win[off:off + CONV_ROWS, :] * cw[j:j + 1, :]
        acc = acc + cb
        mu = jnp.mean(acc, axis=-1, keepdims=True)
        cen = acc - mu
        var = jnp.mean(cen * cen, axis=-1, keepdims=True)
        y = cen * lax.rsqrt(var + EPS) * lg + lb
        y = y * (1.0 / (1.0 + jnp.exp(-y)))
        conv_ref[pl.ds(start, CONV_ROWS), :] = y.astype(conv_ref.dtype)
        return carry

    lax.fori_loop(0, tm // CONV_ROWS, conv_step, 0)

    mixed = (jnp.dot(attn_ref[...], wo_ref[0:ATTN_WIDTH, :], preferred_element_type=jnp.float32)
             + jnp.dot(conv_ref[...], wo_ref[ATTN_WIDTH:, :], preferred_element_type=jnp.float32))
    x1 = x_ref[...] + mixed
    x1_ref[...] = x1
    ms = jnp.mean(x1 * x1, axis=-1, keepdims=True)
    hn_ref[...] = x1 * lax.rsqrt(ms + EPS) * g2_ref[...]


def _mix_out(attn, hg, cw, cb, lg, lb, w_out, x2, g2, seq, tm):
    t, d = x2.shape
    n_s = seq // tm
    hb = tm // HALO
    last_halo = t // HALO - 1
    return pl.pallas_call(
        functools.partial(_mix_out_kernel, n_s=n_s),
        grid=(t // tm,),
        in_specs=[
            pl.BlockSpec((tm, ATTN_WIDTH), lambda i: (i, 0)),
            pl.BlockSpec((tm, CONV_CH), lambda i: (i, 0)),
            pl.BlockSpec((HALO, CONV_CH), lambda i: (jnp.maximum(i * hb - 1, 0), 0)),
            pl.BlockSpec((HALO, CONV_CH), lambda i: (jnp.minimum((i + 1) * hb, last_halo), 0)),
            pl.BlockSpec((CONV_W + 1, CONV_CH), lambda i: (0, 0)),
            pl.BlockSpec((1, CONV_CH), lambda i: (0, 0)),
            pl.BlockSpec((1, CONV_CH), lambda i: (0, 0)),
            pl.BlockSpec((1, CONV_CH), lambda i: (0, 0)),
            pl.BlockSpec((d, d), lambda i: (0, 0)),
            pl.BlockSpec((tm, d), lambda i: (i, 0)),
            pl.BlockSpec((1, d), lambda i: (0, 0)),
        ],
        out_specs=[
            pl.BlockSpec((tm, d), lambda i: (i, 0)),
            pl.BlockSpec((tm, d), lambda i: (i, 0)),
        ],
        out_shape=[
            jax.ShapeDtypeStruct((t, d), jnp.float32),
            jax.ShapeDtypeStruct((t, d), jnp.float32),
        ],
        scratch_shapes=[
            pltpu.VMEM((tm + 2 * HALO, CONV_CH), jnp.float32),
            pltpu.VMEM((tm, CONV_CH), jnp.bfloat16),
        ],
        compiler_params=_cparams(("parallel",)),
        name="mix_out",
    )(attn, hg, hg, hg, cw, cb, lg, lb, w_out, x2, g2)


def _top_rows(s, order, payload, count):
    big = jnp.float32(2 ** 30)
    vals = []
    picks = []
    for _ in range(count):
        m = jnp.max(s, axis=0, keepdims=True)
        first = jnp.min(jnp.where(s == m, order, big), axis=0, keepdims=True)
        chosen = order == first
        if payload is order:
            pick = first
        else:
            pick = jnp.max(jnp.where(chosen, payload, -1.0), axis=0, keepdims=True)
        vals.append(m)
        picks.append(pick)
        s = jnp.where(chosen, _NEG_INF, s)
    return jnp.concatenate(vals, axis=0), jnp.concatenate(picks, axis=0)


def _candidate_blocks():
    blocks = [(0, 0, 16)]
    for a in range(1, 8):
        blocks.append((a, 0, 8))
    return blocks


def _route_kernel(hn_ref, wqt_ref, keys_ref, e_ref, g_ref):
    hn = hn_ref[...].astype(jnp.bfloat16)
    qpt = lax.dot_general(wqt_ref[...], hn, (((1,), (1,)), ((), ())),
                          preferred_element_type=jnp.float32)
    qpt = qpt.astype(jnp.bfloat16)
    tm = hn.shape[0]
    half = PEER_DQ // 2
    k0 = keys_ref[0]
    k1 = keys_ref[1]
    row_iota = lax.broadcasted_iota(jnp.int32, (N_KEYS, LANES), 0).astype(jnp.float32)

    blocks = _candidate_blocks()
    flat_parts = []
    for a, b0, nb in blocks:
        flat_parts.append(a * PEER_TOPK + b0 + lax.broadcasted_iota(jnp.int32, (nb, LANES), 0))
    tail_flat = (8 + lax.broadcasted_iota(jnp.int32, (8, LANES), 0)) * PEER_TOPK
    flat = jnp.concatenate(flat_parts + [tail_flat], axis=0).astype(jnp.float32)

    for h in range(PEER_HEADS):
        q1 = qpt[h * PEER_DQ:h * PEER_DQ + half, :]
        q2 = qpt[h * PEER_DQ + half:(h + 1) * PEER_DQ, :]
        s1 = jnp.dot(k0, q1, preferred_element_type=jnp.float32)
        s2 = jnp.dot(k1, q2, preferred_element_type=jnp.float32)
        for c in range(tm // LANES):
            sl = slice(c * LANES, (c + 1) * LANES)
            v1, i1 = _top_rows(s1[:, sl], row_iota, row_iota, PEER_TOPK)
            v2, i2 = _top_rows(s2[:, sl], row_iota, row_iota, PEER_TOPK)
            cs = []
            ce = []
            for a, b0, nb in blocks:
                cs.append(v1[a:a + 1, :] + v2[b0:b0 + nb, :])
                ce.append(i1[a:a + 1, :] * N_KEYS + i2[b0:b0 + nb, :])
            cs.append(v1[8:16, :] + v2[0:1, :])
            ce.append(i1[8:16, :] * N_KEYS + i2[0:1, :])
            cand_s = jnp.concatenate(cs, axis=0)
            cand_e = jnp.concatenate(ce, axis=0)
            top_s, top_e = _top_rows(cand_s, flat, cand_e, PEER_TOPK)
            ex = jnp.exp(top_s - top_s[0:1, :])
            gates = ex / jnp.sum(ex, axis=0, keepdims=True)
            off = top_e.astype(jnp.int32) * ROW_WORDS
            half_k = PEER_TOPK // 2
            e_ref[h, :, sl] = (off[half_k:, :] << 16) | off[:half_k, :]
            g_ref[h, :, sl] = gates


def _route(hn, wqt, keys, tm):
    t, d = hn.shape
    return pl.pallas_call(
        _route_kernel,
        grid=(t // tm,),
        in_specs=[
            pl.BlockSpec((tm, d), lambda i: (i, 0)),
            pl.BlockSpec(wqt.shape, lambda i: (0, 0)),
            pl.BlockSpec(keys.shape, lambda i: (0, 0, 0)),
        ],
        out_specs=[
            pl.BlockSpec((PEER_HEADS, PEER_TOPK // 2, tm), lambda i: (0, 0, i)),
            pl.BlockSpec((PEER_HEADS, PEER_TOPK, tm), lambda i: (0, 0, i)),
        ],
        out_shape=[
            jax.ShapeDtypeStruct((PEER_HEADS, PEER_TOPK // 2, t), jnp.int32),
            jax.ShapeDtypeStruct((PEER_HEADS, PEER_TOPK, t), jnp.float32),
        ],
        compiler_params=_cparams(("parallel",)),
        name="route",
    )(hn, wqt, keys)


PEER_TOK = 128


def _expert_row(tbl_ref, off):
    words = tbl_ref[pl.ds(pl.multiple_of(off, ROW_WORDS), ROW_WORDS), :]
    return pltpu.bitcast(words, jnp.bfloat16).astype(jnp.float32)


def _expert_offsets(idx_row, head):
    lo, hi = [], []
    for j in range(PEER_TOPK // 2):
        word = idx_row[head * (PEER_TOPK // 2) + j]
        lo.append(word & 0xFFFF)
        hi.append(lax.shift_right_logical(word, 16))
    return lo + hi


def _gelu_tanh(x):
    c = np.float32(np.sqrt(2.0 / np.pi))
    return 0.5 * x * (1.0 + jnp.tanh(c * (x + 0.044715 * (x * x * x))))


def _fold_rows(rows, x, x_swapped, masks):
    lo4, mod4_lo2, even = masks
    a, e, c, g, b, f, d, h = rows

    def level1(p, q):
        t1 = jnp.where(lo4, p, q)
        t2 = pltpu.roll(jnp.where(lo4, q, p), 4, axis=0)
        return t1 * x + t2 * x_swapped

    def level2(u, v):
        uu = u + pltpu.roll(u, 6, axis=0)
        vv = v + pltpu.roll(v, 2, axis=0)
        return jnp.where(mod4_lo2, uu, vv)

    def level3(u, v):
        uu = u + pltpu.roll(u, 7, axis=0)
        vv = v + pltpu.roll(v, 1, axis=0)
        return jnp.where(even, uu, vv)

    return level3(level2(level1(a, b), level1(c, d)), level2(level1(e, f), level1(g, h)))


U_TOKENS_PER_STEP = 8


def _peer_u_kernel(e_ref, x_ref, gate_ref, tbl_ref, w_ref, s_ref, part_a, part_b):
    lane = lax.broadcasted_iota(jnp.int32, (PEER_SEL, PEER_TOK), 1)
    sub = lax.broadcasted_iota(jnp.int32, (SUBLANES, LANES), 0)
    masks = (sub < 4, (sub % 4) < 2, (sub % 2) == 0)

    def fold(t, part_ref):
        e_row = e_ref.at[t]
        x = x_ref[t]
        x_swapped = pltpu.roll(x, 4, axis=0)
        for head in range(PEER_HEADS):
            offs = _expert_offsets(e_row, head)
            for half in range(PEER_TOPK // SUBLANES):
                grp = head * (PEER_TOPK // SUBLANES) + half
                rows = [_expert_row(tbl_ref, offs[half * SUBLANES + j]) for j in range(SUBLANES)]
                part_ref[grp * SUBLANES:(grp + 1) * SUBLANES, :] = _fold_rows(rows, x, x_swapped, masks)

    def finish(t, part_ref):
        s = jnp.sum(part_ref[...], axis=1, keepdims=True)
        s_ref[...] = jnp.where(lane == t, s, s_ref[...])

    s_ref[...] = jnp.zeros(s_ref.shape, jnp.float32)
    part_b[...] = jnp.zeros(part_b.shape, jnp.float32)

    def token_block(i, carry):
        t0 = U_TOKENS_PER_STEP * i
        for j in range(U_TOKENS_PER_STEP):
            cur, prev = (part_a, part_b) if j % 2 == 0 else (part_b, part_a)
            fold(t0 + j, cur)
            finish(t0 + j - 1, prev)
        return carry

    lax.fori_loop(0, PEER_TOK // U_TOKENS_PER_STEP, token_block, 0)
    finish(PEER_TOK - 1, part_b)
    gates = gate_ref[...].reshape(PEER_SEL, PEER_TOK)
    w_ref[...] = gates * _gelu_tanh(s_ref[...])


def _peer_u(e_sm, x3, gates, tbl):
    t = x3.shape[0]
    return pl.pallas_call(
        _peer_u_kernel,
        grid=(t // PEER_TOK,),
        in_specs=[
            pl.BlockSpec((PEER_TOK, PEER_SEL // 2), lambda i: (i, 0), memory_space=pltpu.SMEM),
            pl.BlockSpec((PEER_TOK, SUBLANES, LANES), lambda i: (i, 0, 0)),
            pl.BlockSpec((PEER_HEADS, PEER_TOPK, PEER_TOK), lambda i: (0, 0, i)),
            pl.BlockSpec(tbl.shape, lambda i: (0, 0), pipeline_mode=pl.Buffered(1)),
        ],
        out_specs=pl.BlockSpec((PEER_SEL, PEER_TOK), lambda i: (0, i)),
        out_shape=jax.ShapeDtypeStruct((PEER_SEL, t), jnp.float32),
        scratch_shapes=[
            pltpu.VMEM((PEER_SEL, PEER_TOK), jnp.float32),
            pltpu.VMEM((PEER_SEL, LANES), jnp.float32),
            pltpu.VMEM((PEER_SEL, LANES), jnp.float32),
        ],
        compiler_params=_cparams(("arbitrary",)),
        name="peer_u",
    )(e_sm, x3, gates, tbl)


def _peer_v_kernel(e_ref, wt_ref, x1_ref, fg_ref, tbl_ref, y_ref):
    fg = fg_ref[...]
    d_model = SUBLANES * LANES
    lane = lax.broadcasted_iota(jnp.int32, (PEER_SEL, PEER_TOK), 1)

    def spread(t):
        col = jnp.sum(jnp.where(lane == t, wt_ref[...], 0.0), axis=1, keepdims=True)
        return jnp.broadcast_to(col, (PEER_SEL, LANES))

    def token(t, wb):
        wb_next = spread(jnp.minimum(t + 1, PEER_TOK - 1))
        e_row = e_ref.at[t]
        acc = x1_ref[t]
        for head in range(PEER_HEADS):
            offs = _expert_offsets(e_row, head)
            for half in range(PEER_TOPK // SUBLANES):
                terms = []
                for j in range(SUBLANES):
                    slot = half * SUBLANES + j
                    k = head * PEER_TOPK + slot
                    terms.append(wb[k:k + 1, :] * _expert_row(tbl_ref, offs[slot]))
                while len(terms) > 1:
                    terms = [terms[i] + terms[i + 1] for i in range(0, len(terms), 2)]
                acc = acc + terms[0]
        y_ref[t] = acc
        return wb_next

    lax.fori_loop(0, PEER_TOK, token, spread(0))
    x2 = y_ref[...]
    ms = jnp.sum(x2 * x2, axis=(1, 2), keepdims=True) * (1.0 / d_model)
    y_ref[...] = x2 * lax.rsqrt(ms + EPS) * fg


def _peer_v(e_sm, wt, x13, fg, tbl):
    t = x13.shape[0]
    return pl.pallas_call(
        _peer_v_kernel,
        grid=(t // PEER_TOK,),
        in_specs=[
            pl.BlockSpec((PEER_TOK, PEER_SEL // 2), lambda i: (i, 0), memory_space=pltpu.SMEM),
            pl.BlockSpec((PEER_SEL, PEER_TOK), lambda i: (0, i)),
            pl.BlockSpec((PEER_TOK, SUBLANES, LANES), lambda i: (i, 0, 0)),
            pl.BlockSpec((SUBLANES, LANES), lambda i: (0, 0)),
            pl.BlockSpec(tbl.shape, lambda i: (0, 0), pipeline_mode=pl.Buffered(1)),
        ],
        out_specs=pl.BlockSpec((PEER_TOK, SUBLANES, LANES), lambda i: (i, 0, 0)),
        out_shape=jax.ShapeDtypeStruct((t, SUBLANES, LANES), jnp.float32),
        compiler_params=_cparams(("arbitrary",)),
        name="peer_v",
    )(e_sm, wt, x13, fg, tbl)


def _pack_table(tbl):
    n, d = tbl.shape
    assert d == ROW_TILE * LANES
    tb = tbl.astype(jnp.bfloat16)
    pairs = jnp.stack([tb[:, :d // 2], tb[:, d // 2:]], axis=-1)
    return lax.bitcast_convert_type(pairs, jnp.int32).reshape(n * ROW_WORDS, LANES)


def _tile_rows(a):
    t, d = a.shape
    return a.reshape(t, 2, ROW_WORDS, LANES).transpose(0, 2, 1, 3).reshape(t, ROW_TILE, LANES)


def _untile_rows(a):
    t = a.shape[0]
    return a.reshape(t, ROW_WORDS, 2, LANES).transpose(0, 2, 1, 3).reshape(t, ROW_TILE * LANES)


def _rope_tables(seq):
    pos = jnp.arange(seq)
    row = (pos // GRID_W).astype(jnp.float32)
    col = (pos % GRID_W).astype(jnp.float32)
    n_pairs = ROPE_AXIS_DIM // 2
    inv_freq = ROPE_THETA ** (-jnp.arange(n_pairs, dtype=jnp.float32) / n_pairs)
    ang_r = row[:, None] * inv_freq[None, :]
    ang_c = col[:, None] * inv_freq[None, :]
    cos = jnp.concatenate([jnp.cos(ang_r), jnp.cos(ang_r), jnp.cos(ang_c), jnp.cos(ang_c)], axis=1)
    sin = jnp.concatenate([-jnp.sin(ang_r), jnp.sin(ang_r), -jnp.sin(ang_c), jnp.sin(ang_c)], axis=1)
    reps = LANES // HEAD_DIM
    return jnp.tile(cos, (1, reps)), jnp.tile(sin, (1, reps))


def kernel(x, norm1_g, w_in, q_norm_g, k_norm_g, conv_dw, conv_b, conv_ln_g, conv_ln_b, w_out,
           norm2_g, peer_wq, peer_keys, peer_u, peer_v, final_g):
    b, s, d = x.shape
    t = b * s
    assert d == SUBLANES * LANES and s % GRID_W == 0 and N_KV_HEADS == 2
    tm = min(512, s)
    tq = min(512, s)
    tk = min(1024, s)
    tr = min(256, s)
    assert s % tm == 0 and t % PEER_TOK == 0

    f32 = jnp.float32
    bf16 = jnp.bfloat16
    x2 = x.reshape(t, d)
    cos_t, sin_t = _rope_tables(s)
    head_id = jnp.arange(ATTN_WIDTH) // HEAD_DIM
    ones_bd = (head_id[:, None] == head_id[None, :]).astype(bf16)

    qt, k, vt, hg = _in_proj(
        x2, norm1_g.reshape(1, d), w_in.astype(bf16),
        jnp.tile(q_norm_g, N_HEADS).reshape(1, ATTN_WIDTH),
        jnp.tile(k_norm_g, N_KV_HEADS).reshape(1, KV_WIDTH),
        cos_t, sin_t, ones_bd, b, s, tm)

    attn = _attention(qt, k, vt, b, s, tq, tk)

    cw = jnp.concatenate([conv_dw.reshape(CONV_W, CONV_CH), jnp.zeros((1, CONV_CH), f32)], axis=0)
    x1, hn = _mix_out(attn, hg, cw, conv_b.reshape(1, CONV_CH), conv_ln_g.reshape(1, CONV_CH),
                      conv_ln_b.reshape(1, CONV_CH), w_out.astype(bf16), x2,
                      norm2_g.reshape(1, d), s, tm)

    e, gates = _route(hn, peer_wq.T.astype(bf16), peer_keys.astype(bf16), tr)

    e_sm = e.reshape(PEER_SEL // 2, t).T
    w = _peer_u(e_sm, _tile_rows(hn), gates, _pack_table(peer_u))
    y = _peer_v(e_sm, w, _tile_rows(x1), _tile_rows(final_g.reshape(1, d))[0], _pack_table(peer_v))
    return _untile_rows(y).reshape(b, s, d)
```

```python
import functools

import jax
import jax.numpy as jnp
import numpy as np
from jax import lax
from jax.experimental import pallas as pl
from jax.experimental.pallas import tpu as pltpu

HEAD_DIM = 64
N_HEADS = 8
N_KV_HEADS = 2
GROUP = N_HEADS // N_KV_HEADS
ATTN_WIDTH = N_HEADS * HEAD_DIM
KV_WIDTH = N_KV_HEADS * HEAD_DIM
CONV_CH = 512
CONV_W = 31
CONV_PAD = (CONV_W - 1) // 2
GRID_W = 64
ROPE_THETA = 10000.0
ROPE_AXIS_DIM = HEAD_DIM // 2
N_KEYS = 128
PEER_HEADS = 8
PEER_DQ = 256
PEER_TOPK = 16
PEER_SEL = PEER_HEADS * PEER_TOPK
EPS = 1e-6

LANES = 128
SUBLANES = 8
HALO = 16
BF16_ROWS = 16
V_EXT = HEAD_DIM + BF16_ROWS
LOG2_E = 1.4426950408889634
VMEM_LIMIT = 56 * 1024 * 1024

ROW_TILE = 8
ROW_WORDS = ROW_TILE // 2
TILE_CHUNK = tuple((r % 2) * ROW_WORDS + r // 2 for r in range(ROW_TILE))

_NEG_INF = float("-inf")


def _cparams(sem, vmem=VMEM_LIMIT):
    return pltpu.CompilerParams(dimension_semantics=sem, vmem_limit_bytes=vmem)


def _group_mean_sq(v, ones_blockdiag):
    sq = v * v
    hi = sq.astype(jnp.bfloat16)
    lo = (sq - hi.astype(jnp.float32)).astype(jnp.bfloat16)
    tot = (jnp.dot(hi, ones_blockdiag, preferred_element_type=jnp.float32)
           + jnp.dot(lo, ones_blockdiag, preferred_element_type=jnp.float32))
    return tot * (1.0 / HEAD_DIM)


def _rope(v, cos, sin_signed, lane_in_pair_lo):
    width = v.shape[-1]
    up = pltpu.roll(v, width - ROPE_AXIS_DIM // 2, axis=1)
    down = pltpu.roll(v, ROPE_AXIS_DIM // 2, axis=1)
    partner = jnp.where(lane_in_pair_lo, up, down)
    return v * cos + partner * sin_signed


def _in_proj_kernel(x_ref, g1_ref, w_ref, qg_ref, kg_ref, cos_ref, sin_ref, ones_ref,
                    qt_ref, k_ref, vt_ref, hg_ref):
    x = x_ref[...]
    ms = jnp.mean(x * x, axis=-1, keepdims=True)
    h = (x * lax.rsqrt(ms + EPS) * g1_ref[...]).astype(jnp.bfloat16)
    p = jnp.dot(h, w_ref[...], preferred_element_type=jnp.float32)
    o1 = ATTN_WIDTH
    o2 = o1 + KV_WIDTH
    o3 = o2 + KV_WIDTH
    o4 = o3 + CONV_CH
    q = p[:, :o1]
    k = p[:, o1:o2]
    v = p[:, o2:o3]
    a = p[:, o3:o4]
    gate = p[:, o4:]

    ones_bd = ones_ref[...]
    qn = q * lax.rsqrt(_group_mean_sq(q, ones_bd) + EPS) * qg_ref[...]
    kn = k * lax.rsqrt(_group_mean_sq(k, ones_bd[:KV_WIDTH, :KV_WIDTH]) + EPS) * kg_ref[...]

    cos2 = cos_ref[...]
    sin2 = sin_ref[...]
    cos_q = jnp.concatenate([cos2] * (ATTN_WIDTH // LANES), axis=1)
    sin_q = jnp.concatenate([sin2] * (ATTN_WIDTH // LANES), axis=1)
    lane_q = lax.broadcasted_iota(jnp.int32, q.shape, 1)
    lane_k = lax.broadcasted_iota(jnp.int32, k.shape, 1)
    half = ROPE_AXIS_DIM // 2
    qr = _rope(qn, cos_q, sin_q, (lane_q % ROPE_AXIS_DIM) < half) * (HEAD_DIM ** -0.5 * LOG2_E)
    kr = _rope(kn, cos2, sin2, (lane_k % ROPE_AXIS_DIM) < half)

    qt = qr.T.astype(qt_ref.dtype)
    zeros = jnp.zeros((HEAD_DIM, qt.shape[1]), qt_ref.dtype)
    for hh in range(N_HEADS):
        g = hh // GROUP
        piece = qt[hh * HEAD_DIM:(hh + 1) * HEAD_DIM, :]
        qt_ref[0, hh, g * HEAD_DIM:(g + 1) * HEAD_DIM, :] = piece
        qt_ref[0, hh, (1 - g) * HEAD_DIM:(2 - g) * HEAD_DIM, :] = zeros
    k_ref[...] = kr.astype(k_ref.dtype)
    vt = v.T.astype(vt_ref.dtype)
    pad_row = lax.broadcasted_iota(jnp.int32, (V_EXT - HEAD_DIM, vt.shape[1]), 0)
    ones_pad = jnp.where(pad_row == 0, 1.0, 0.0).astype(vt_ref.dtype)
    for g in range(N_KV_HEADS):
        vt_ref[0, g, 0:HEAD_DIM, :] = vt[g * HEAD_DIM:(g + 1) * HEAD_DIM, :]
        vt_ref[0, g, HEAD_DIM:V_EXT, :] = ones_pad
    hg_ref[...] = a * (1.0 / (1.0 + jnp.exp(-gate)))


def _in_proj(x2, g1, w_in, qg, kg, cos_t, sin_t, ones_bd, batch, seq, tm):
    t, d = x2.shape
    n_s = seq // tm
    in_width = w_in.shape[1]
    return pl.pallas_call(
        _in_proj_kernel,
        grid=(t // tm,),
        in_specs=[
            pl.BlockSpec((tm, d), lambda i: (i, 0)),
            pl.BlockSpec((1, d), lambda i: (0, 0)),
            pl.BlockSpec((d, in_width), lambda i: (0, 0)),
            pl.BlockSpec((1, ATTN_WIDTH), lambda i: (0, 0)),
            pl.BlockSpec((1, KV_WIDTH), lambda i: (0, 0)),
            pl.BlockSpec((tm, LANES), lambda i: (i % n_s, 0)),
            pl.BlockSpec((tm, LANES), lambda i: (i % n_s, 0)),
            pl.BlockSpec((ATTN_WIDTH, ATTN_WIDTH), lambda i: (0, 0)),
        ],
        out_specs=[
            pl.BlockSpec((1, N_HEADS, KV_WIDTH, tm), lambda i: (i // n_s, 0, 0, i % n_s)),
            pl.BlockSpec((tm, KV_WIDTH), lambda i: (i, 0)),
            pl.BlockSpec((1, N_KV_HEADS, V_EXT, tm), lambda i: (i // n_s, 0, 0, i % n_s)),
            pl.BlockSpec((tm, CONV_CH), lambda i: (i, 0)),
        ],
        out_shape=[
            jax.ShapeDtypeStruct((batch, N_HEADS, KV_WIDTH, seq), jnp.bfloat16),
            jax.ShapeDtypeStruct((t, KV_WIDTH), jnp.bfloat16),
            jax.ShapeDtypeStruct((batch, N_KV_HEADS, V_EXT, seq), jnp.bfloat16),
            jax.ShapeDtypeStruct((t, CONV_CH), jnp.float32),
        ],
        compiler_params=_cparams(("parallel",)),
        name="in_proj",
    )(x2, g1, w_in, qg, kg, cos_t, sin_t, ones_bd)


SCORES_AHEAD = 2


def _attn_kernel(qt_ref, k_ref, vt_ref, o_ref, m_ref, acc_ref):
    ki = pl.program_id(2)

    @pl.when(ki == 0)
    def _():
        m_ref[...] = jnp.full(m_ref.shape, _NEG_INF, jnp.float32)
        acc_ref[...] = jnp.zeros(acc_ref.shape, jnp.float32)

    k = k_ref[...]

    def scores(h):
        return jnp.dot(k, qt_ref[0, h], preferred_element_type=jnp.float32)

    ahead = [scores(h) for h in range(SCORES_AHEAD)]
    for h in range(N_HEADS):
        g = h // GROUP
        s = ahead.pop(0)
        if h + SCORES_AHEAD < N_HEADS:
            ahead.append(scores(h + SCORES_AHEAD))
        m_prev = m_ref[h:h + 1, :]
        m_new = jnp.maximum(m_prev, jnp.max(s, axis=0, keepdims=True))
        alpha = jnp.exp2(m_prev - m_new)
        p = jnp.exp2(s - m_new).astype(jnp.bfloat16)
        pv = jnp.dot(vt_ref[0, g], p, preferred_element_type=jnp.float32)
        acc_ref[h] = alpha * acc_ref[h] + pv
        m_ref[h:h + 1, :] = m_new

    @pl.when(ki == pl.num_programs(2) - 1)
    def _():
        outs = []
        for h in range(N_HEADS):
            acc = acc_ref[h]
            outs.append(acc[:HEAD_DIM, :] / acc[HEAD_DIM:HEAD_DIM + 1, :])
        o_ref[...] = jnp.concatenate(outs, axis=0).T.astype(o_ref.dtype)


def _attention(qt, k, vt, batch, seq, tq, tk):
    t = k.shape[0]
    nq = seq // tq
    nk = seq // tk
    return pl.pallas_call(
        _attn_kernel,
        grid=(batch, nq, nk),
        in_specs=[
            pl.BlockSpec((1, N_HEADS, KV_WIDTH, tq), lambda b, qi, ki: (b, 0, 0, qi)),
            pl.BlockSpec((tk, KV_WIDTH), lambda b, qi, ki: (b * nk + ki, 0)),
            pl.BlockSpec((1, N_KV_HEADS, V_EXT, tk), lambda b, qi, ki: (b, 0, 0, ki)),
        ],
        out_specs=pl.BlockSpec((tq, ATTN_WIDTH), lambda b, qi, ki: (b * nq + qi, 0)),
        out_shape=jax.ShapeDtypeStruct((t, ATTN_WIDTH), jnp.bfloat16),
        scratch_shapes=[
            pltpu.VMEM((N_HEADS, tq), jnp.float32),
            pltpu.VMEM((N_HEADS, V_EXT, tq), jnp.float32),
        ],
        compiler_params=_cparams(("parallel", "parallel", "arbitrary")),
        name="attention",
    )(qt, k, vt)


CONV_ROWS = 32


def _mix_out_kernel(attn_ref, hg_ref, prev_ref, next_ref, cw_ref, cb_ref, lg_ref, lb_ref,
                    wo_ref, x_ref, g2_ref, x1t_ref, hn_ref, hnt_ref, ext_ref, conv_ref, *, n_s):
    tm = hg_ref.shape[0]
    si = pl.program_id(0) % n_s
    keep_prev = (si > 0).astype(jnp.float32)
    keep_next = (si < n_s - 1).astype(jnp.float32)
    ext_ref[0:HALO, :] = prev_ref[...] * keep_prev
    ext_ref[HALO:HALO + tm, :] = hg_ref[...]
    ext_ref[HALO + tm:HALO + tm + HALO, :] = next_ref[...] * keep_next

    cw = cw_ref[...]
    cb = cb_ref[...]
    lg = lg_ref[...]
    lb = lb_ref[...]

    def conv_step(c, carry):
        start = pl.multiple_of(c * CONV_ROWS, CONV_ROWS)
        win = ext_ref[pl.ds(start, CONV_ROWS + 2 * HALO), :]
        acc = jnp.zeros((CONV_ROWS, CONV_CH), jnp.float32)
        for j in range(CONV_W):
            off = HALO - CONV_PAD + j
            acc = acc + win[off:off + CONV_ROWS, :] * cw[j:j + 1, :]
        acc = acc + cb
        mu = jnp.mean(acc, axis=-1, keepdims=True)
        cen = acc - mu
        var = jnp.mean(cen * cen, axis=-1, keepdims=True)
        y = cen * lax.rsqrt(var + EPS) * lg + lb
        y = y * (1.0 / (1.0 + jnp.exp(-y)))
        conv_ref[pl.ds(start, CONV_ROWS), :] = y.astype(conv_ref.dtype)
        return carry

    lax.fori_loop(0, tm // CONV_ROWS, conv_step, 0)

    mixed = (jnp.dot(attn_ref[...], wo_ref[0:ATTN_WIDTH, :], preferred_element_type=jnp.float32)
             + jnp.dot(conv_ref[...], wo_ref[ATTN_WIDTH:, :], preferred_element_type=jnp.float32))
    x1 = x_ref[...] + mixed
    ms = jnp.mean(x1 * x1, axis=-1, keepdims=True)
    hn = x1 * lax.rsqrt(ms + EPS) * g2_ref[...]
    hn_ref[...] = hn
    for r, c in enumerate(TILE_CHUNK):
        x1t_ref[:, r, :] = x1[:, c * LANES:(c + 1) * LANES]
        hnt_ref[:, r, :] = hn[:, c * LANES:(c + 1) * LANES]


def _mix_out(attn, hg, cw, cb, lg, lb, w_out, x2, g2, seq, tm):
    t, d = x2.shape
    n_s = seq // tm
    hb = tm // HALO
    last_halo = t // HALO - 1
    return pl.pallas_call(
        functools.partial(_mix_out_kernel, n_s=n_s),
        grid=(t // tm,),
        in_specs=[
            pl.BlockSpec((tm, ATTN_WIDTH), lambda i: (i, 0)),
            pl.BlockSpec((tm, CONV_CH), lambda i: (i, 0)),
            pl.BlockSpec((HALO, CONV_CH), lambda i: (jnp.maximum(i * hb - 1, 0), 0)),
            pl.BlockSpec((HALO, CONV_CH), lambda i: (jnp.minimum((i + 1) * hb, last_halo), 0)),
            pl.BlockSpec((CONV_W + 1, CONV_CH), lambda i: (0, 0)),
            pl.BlockSpec((1, CONV_CH), lambda i: (0, 0)),
            pl.BlockSpec((1, CONV_CH), lambda i: (0, 0)),
            pl.BlockSpec((1, CONV_CH), lambda i: (0, 0)),
            pl.BlockSpec((d, d), lambda i: (0, 0)),
            pl.BlockSpec((tm, d), lambda i: (i, 0)),
            pl.BlockSpec((1, d), lambda i: (0, 0)),
        ],
        out_specs=[
            pl.BlockSpec((tm, ROW_TILE, LANES), lambda i: (i, 0, 0)),
            pl.BlockSpec((tm, d), lambda i: (i, 0)),
            pl.BlockSpec((tm, ROW_TILE, LANES), lambda i: (i, 0, 0)),
        ],
        out_shape=[
            jax.ShapeDtypeStruct((t, ROW_TILE, LANES), jnp.float32),
            jax.ShapeDtypeStruct((t, d), jnp.float32),
            jax.ShapeDtypeStruct((t, ROW_TILE, LANES), jnp.float32),
        ],
        scratch_shapes=[
            pltpu.VMEM((tm + 2 * HALO, CONV_CH), jnp.float32),
            pltpu.VMEM((tm, CONV_CH), jnp.bfloat16),
        ],
        compiler_params=_cparams(("parallel",)),
        name="mix_out",
    )(attn, hg, hg, hg, cw, cb, lg, lb, w_out, x2, g2)


def _top_rows(s, order, payload, count):
    big = jnp.float32(2 ** 30)
    vals = []
    picks = []
    for _ in range(count):
        m = jnp.max(s, axis=0, keepdims=True)
        first = jnp.min(jnp.where(s == m, order, big), axis=0, keepdims=True)
        chosen = order == first
        if payload is order:
            pick = first
        else:
            pick = jnp.max(jnp.where(chosen, payload, -1.0), axis=0, keepdims=True)
        vals.append(m)
        picks.append(pick)
        s = jnp.where(chosen, _NEG_INF, s)
    return jnp.concatenate(vals, axis=0), jnp.concatenate(picks, axis=0)


def _candidate_blocks():
    blocks = [(0, 0, 16)]
    for a in range(1, 8):
        blocks.append((a, 0, 8))
    return blocks


def _route_kernel(hn_ref, wqt_ref, keys_ref, e_ref, g_ref, words_ref):
    hn = hn_ref[...].astype(jnp.bfloat16)
    qpt = lax.dot_general(wqt_ref[...], hn, (((1,), (1,)), ((), ())),
                          preferred_element_type=jnp.float32)
    qpt = qpt.astype(jnp.bfloat16)
    tm = hn.shape[0]
    half = PEER_DQ // 2
    k0 = keys_ref[0]
    k1 = keys_ref[1]
    row_iota = lax.broadcasted_iota(jnp.int32, (N_KEYS, LANES), 0).astype(jnp.float32)

    blocks = _candidate_blocks()
    flat_parts = []
    for a, b0, nb in blocks:
        flat_parts.append(a * PEER_TOPK + b0 + lax.broadcasted_iota(jnp.int32, (nb, LANES), 0))
    tail_flat = (8 + lax.broadcasted_iota(jnp.int32, (8, LANES), 0)) * PEER_TOPK
    flat = jnp.concatenate(flat_parts + [tail_flat], axis=0).astype(jnp.float32)

    for h in range(PEER_HEADS):
        q1 = qpt[h * PEER_DQ:h * PEER_DQ + half, :]
        q2 = qpt[h * PEER_DQ + half:(h + 1) * PEER_DQ, :]
        s1 = jnp.dot(k0, q1, preferred_element_type=jnp.float32)
        s2 = jnp.dot(k1, q2, preferred_element_type=jnp.float32)
        for c in range(tm // LANES):
            sl = slice(c * LANES, (c + 1) * LANES)
            v1, i1 = _top_rows(s1[:, sl], row_iota, row_iota, PEER_TOPK)
            v2, i2 = _top_rows(s2[:, sl], row_iota, row_iota, PEER_TOPK)
            cs = []
            ce = []
            for a, b0, nb in blocks:
                cs.append(v1[a:a + 1, :] + v2[b0:b0 + nb, :])
                ce.append(i1[a:a + 1, :] * N_KEYS + i2[b0:b0 + nb, :])
            cs.append(v1[8:16, :] + v2[0:1, :])
            ce.append(i1[8:16, :] * N_KEYS + i2[0:1, :])
            cand_s = jnp.concatenate(cs, axis=0)
            cand_e = jnp.concatenate(ce, axis=0)
            top_s, top_e = _top_rows(cand_s, flat, cand_e, PEER_TOPK)
            ex = jnp.exp(top_s - top_s[0:1, :])
            gates = ex / jnp.sum(ex, axis=0, keepdims=True)
            off = top_e.astype(jnp.int32) * ROW_WORDS
            half_k = PEER_TOPK // 2
            words_ref[h * half_k:(h + 1) * half_k, sl] = (off[half_k:, :] << 16) | off[:half_k, :]
            g_ref[h, :, sl] = gates
    e_ref[...] = words_ref[...].T


def _route(hn, wqt, keys, tm):
    t, d = hn.shape
    return pl.pallas_call(
        _route_kernel,
        grid=(t // tm,),
        in_specs=[
            pl.BlockSpec((tm, d), lambda i: (i, 0)),
            pl.BlockSpec(wqt.shape, lambda i: (0, 0)),
            pl.BlockSpec(keys.shape, lambda i: (0, 0, 0)),
        ],
        out_specs=[
            pl.BlockSpec((tm, PEER_SEL // 2), lambda i: (i, 0)),
            pl.BlockSpec((PEER_HEADS, PEER_TOPK, tm), lambda i: (0, 0, i)),
        ],
        out_shape=[
            jax.ShapeDtypeStruct((t, PEER_SEL // 2), jnp.int32),
            jax.ShapeDtypeStruct((PEER_HEADS, PEER_TOPK, t), jnp.float32),
        ],
        scratch_shapes=[pltpu.VMEM((PEER_SEL // 2, tm), jnp.int32)],
        compiler_params=_cparams(("parallel",)),
        name="route",
    )(hn, wqt, keys)


PEER_TOK = 128


def _expert_row(tbl_ref, off):
    words = tbl_ref[pl.ds(pl.multiple_of(off, ROW_WORDS), ROW_WORDS), :]
    return pltpu.bitcast(words, jnp.bfloat16).astype(jnp.float32)


def _expert_offsets(idx_row, head):
    lo, hi = [], []
    for j in range(PEER_TOPK // 2):
        word = idx_row[head * (PEER_TOPK // 2) + j]
        lo.append(word & 0xFFFF)
        hi.append(lax.shift_right_logical(word, 16))
    return lo + hi


def _gelu_tanh(x):
    c = np.float32(np.sqrt(2.0 / np.pi))
    return 0.5 * x * (1.0 + jnp.tanh(c * (x + 0.044715 * (x * x * x))))


def _fold_rows(rows, x, x_swapped, masks):
    lo4, mod4_lo2, even = masks
    a, e, c, g, b, f, d, h = rows

    def level1(p, q):
        t1 = jnp.where(lo4, p, q)
        t2 = pltpu.roll(jnp.where(lo4, q, p), 4, axis=0)
        return t1 * x + t2 * x_swapped

    def level2(u, v):
        uu = u + pltpu.roll(u, 6, axis=0)
        vv = v + pltpu.roll(v, 2, axis=0)
        return jnp.where(mod4_lo2, uu, vv)

    def level3(u, v):
        uu = u + pltpu.roll(u, 7, axis=0)
        vv = v + pltpu.roll(v, 1, axis=0)
        return jnp.where(even, uu, vv)

    return level3(level2(level1(a, b), level1(c, d)), level2(level1(e, f), level1(g, h)))


U_TOKENS_PER_STEP = 8


def _peer_u_kernel(e_ref, x_ref, gate_ref, tbl_ref, w_ref, s_ref, part_a, part_b):
    lane = lax.broadcasted_iota(jnp.int32, (PEER_SEL, PEER_TOK), 1)
    sub = lax.broadcasted_iota(jnp.int32, (SUBLANES, LANES), 0)
    masks = (sub < 4, (sub % 4) < 2, (sub % 2) == 0)

    def fold(t, part_ref):
        e_row = e_ref.at[t]
        x = x_ref[t]
        x_swapped = pltpu.roll(x, 4, axis=0)
        for head in range(PEER_HEADS):
            offs = _expert_offsets(e_row, head)
            for half in range(PEER_TOPK // SUBLANES):
                grp = head * (PEER_TOPK // SUBLANES) + half
                rows = [_expert_row(tbl_ref, offs[half * SUBLANES + j]) for j in range(SUBLANES)]
                part_ref[grp * SUBLANES:(grp + 1) * SUBLANES, :] = _fold_rows(rows, x, x_swapped, masks)

    def finish(t, part_ref):
        s = jnp.sum(part_ref[...], axis=1, keepdims=True)
        s_ref[...] = jnp.where(lane == t, s, s_ref[...])

    s_ref[...] = jnp.zeros(s_ref.shape, jnp.float32)
    part_b[...] = jnp.zeros(part_b.shape, jnp.float32)

    def token_block(i, carry):
        t0 = U_TOKENS_PER_STEP * i
        for j in range(U_TOKENS_PER_STEP):
            cur, prev = (part_a, part_b) if j % 2 == 0 else (part_b, part_a)
            fold(t0 + j, cur)
            finish(t0 + j - 1, prev)
        return carry

    lax.fori_loop(0, PEER_TOK // U_TOKENS_PER_STEP, token_block, 0)
    finish(PEER_TOK - 1, part_b)
    gates = gate_ref[...].reshape(PEER_SEL, PEER_TOK)
    w_ref[...] = gates * _gelu_tanh(s_ref[...])


def _peer_u(e_sm, x3, gates, tbl):
    t = x3.shape[0]
    return pl.pallas_call(
        _peer_u_kernel,
        grid=(t // PEER_TOK,),
        in_specs=[
            pl.BlockSpec((PEER_TOK, PEER_SEL // 2), lambda i: (i, 0), memory_space=pltpu.SMEM),
            pl.BlockSpec((PEER_TOK, SUBLANES, LANES), lambda i: (i, 0, 0)),
            pl.BlockSpec((PEER_HEADS, PEER_TOPK, PEER_TOK), lambda i: (0, 0, i)),
            pl.BlockSpec(tbl.shape, lambda i: (0, 0), pipeline_mode=pl.Buffered(1)),
        ],
        out_specs=pl.BlockSpec((PEER_SEL, PEER_TOK), lambda i: (0, i)),
        out_shape=jax.ShapeDtypeStruct((PEER_SEL, t), jnp.float32),
        scratch_shapes=[
            pltpu.VMEM((PEER_SEL, PEER_TOK), jnp.float32),
            pltpu.VMEM((PEER_SEL, LANES), jnp.float32),
            pltpu.VMEM((PEER_SEL, LANES), jnp.float32),
        ],
        compiler_params=_cparams(("arbitrary",)),
        name="peer_u",
    )(e_sm, x3, gates, tbl)


def _peer_v_kernel(e_ref, wt_ref, x1_ref, fg_ref, tbl_ref, y_ref):
    fg = fg_ref[...]
    d_model = SUBLANES * LANES
    lane = lax.broadcasted_iota(jnp.int32, (PEER_SEL, PEER_TOK), 1)

    def spread(t):
        col = jnp.sum(jnp.where(lane == t, wt_ref[...], 0.0), axis=1, keepdims=True)
        return jnp.broadcast_to(col, (PEER_SEL, LANES))

    def token(t, wb):
        wb_next = spread(jnp.minimum(t + 1, PEER_TOK - 1))
        e_row = e_ref.at[t]
        acc = x1_ref[t]
        for head in range(PEER_HEADS):
            offs = _expert_offsets(e_row, head)
            for half in range(PEER_TOPK // SUBLANES):
                terms = []
                for j in range(SUBLANES):
                    slot = half * SUBLANES + j
                    k = head * PEER_TOPK + slot
                    terms.append(wb[k:k + 1, :] * _expert_row(tbl_ref, offs[slot]))
                while len(terms) > 1:
                    terms = [terms[i] + terms[i + 1] for i in range(0, len(terms), 2)]
                acc = acc + terms[0]
        y_ref[t] = acc
        return wb_next

    lax.fori_loop(0, PEER_TOK, token, spread(0))
    x2 = y_ref[...]
    ms = jnp.sum(x2 * x2, axis=(1, 2), keepdims=True) * (1.0 / d_model)
    y_ref[...] = x2 * lax.rsqrt(ms + EPS) * fg


def _peer_v(e_sm, wt, x13, fg, tbl):
    t = x13.shape[0]
    return pl.pallas_call(
        _peer_v_kernel,
        grid=(t // PEER_TOK,),
        in_specs=[
            pl.BlockSpec((PEER_TOK, PEER_SEL // 2), lambda i: (i, 0), memory_space=pltpu.SMEM),
            pl.BlockSpec((PEER_SEL, PEER_TOK), lambda i: (0, i)),
            pl.BlockSpec((PEER_TOK, SUBLANES, LANES), lambda i: (i, 0, 0)),
            pl.BlockSpec((SUBLANES, LANES), lambda i: (0, 0)),
            pl.BlockSpec(tbl.shape, lambda i: (0, 0), pipeline_mode=pl.Buffered(1)),
        ],
        out_specs=pl.BlockSpec((PEER_TOK, SUBLANES, LANES), lambda i: (i, 0, 0)),
        out_shape=jax.ShapeDtypeStruct((t, SUBLANES, LANES), jnp.float32),
        compiler_params=_cparams(("arbitrary",)),
        name="peer_v",
    )(e_sm, wt, x13, fg, tbl)


def _pack_table(tbl):
    n, d = tbl.shape
    assert d == ROW_TILE * LANES
    tb = tbl.astype(jnp.bfloat16)
    pairs = jnp.stack([tb[:, :d // 2], tb[:, d // 2:]], axis=-1)
    return lax.bitcast_convert_type(pairs, jnp.int32).reshape(n * ROW_WORDS, LANES)


def _tile_rows(a):
    t, d = a.shape
    return a.reshape(t, 2, ROW_WORDS, LANES).transpose(0, 2, 1, 3).reshape(t, ROW_TILE, LANES)


def _untile_rows(a):
    t = a.shape[0]
    return a.reshape(t, ROW_WORDS, 2, LANES).transpose(0, 2, 1, 3).reshape(t, ROW_TILE * LANES)


def _rope_tables(seq):
    pos = jnp.arange(seq)
    row = (pos // GRID_W).astype(jnp.float32)
    col = (pos % GRID_W).astype(jnp.float32)
    n_pairs = ROPE_AXIS_DIM // 2
    inv_freq = ROPE_THETA ** (-jnp.arange(n_pairs, dtype=jnp.float32) / n_pairs)
    ang_r = row[:, None] * inv_freq[None, :]
    ang_c = col[:, None] * inv_freq[None, :]
    cos = jnp.concatenate([jnp.cos(ang_r), jnp.cos(ang_r), jnp.cos(ang_c), jnp.cos(ang_c)], axis=1)
    sin = jnp.concatenate([-jnp.sin(ang_r), jnp.sin(ang_r), -jnp.sin(ang_c), jnp.sin(ang_c)], axis=1)
    reps = LANES // HEAD_DIM
    return jnp.tile(cos, (1, reps)), jnp.tile(sin, (1, reps))


def kernel(x, norm1_g, w_in, q_norm_g, k_norm_g, conv_dw, conv_b, conv_ln_g, conv_ln_b, w_out,
           norm2_g, peer_wq, peer_keys, peer_u, peer_v, final_g):
    b, s, d = x.shape
    t = b * s
    assert d == SUBLANES * LANES and s % GRID_W == 0 and N_KV_HEADS == 2
    tm = min(512, s)
    tq = min(512, s)
    tk = min(1024, s)
    tr = min(256, s)
    assert s % tm == 0 and t % PEER_TOK == 0

    f32 = jnp.float32
    bf16 = jnp.bfloat16
    x2 = x.reshape(t, d)
    cos_t, sin_t = _rope_tables(s)
    head_id = jnp.arange(ATTN_WIDTH) // HEAD_DIM
    ones_bd = (head_id[:, None] == head_id[None, :]).astype(bf16)

    qt, k, vt, hg = _in_proj(
        x2, norm1_g.reshape(1, d), w_in.astype(bf16),
        jnp.tile(q_norm_g, N_HEADS).reshape(1, ATTN_WIDTH),
        jnp.tile(k_norm_g, N_KV_HEADS).reshape(1, KV_WIDTH),
        cos_t, sin_t, ones_bd, b, s, tm)

    attn = _attention(qt, k, vt, b, s, tq, tk)

    cw = jnp.concatenate([conv_dw.reshape(CONV_W, CONV_CH), jnp.zeros((1, CONV_CH), f32)], axis=0)
    x1t, hn, hnt = _mix_out(attn, hg, cw, conv_b.reshape(1, CONV_CH), conv_ln_g.reshape(1, CONV_CH),
                      conv_ln_b.reshape(1, CONV_CH), w_out.astype(bf16), x2,
                      norm2_g.reshape(1, d), s, tm)

    e_sm, gates = _route(hn, peer_wq.T.astype(bf16), peer_keys.astype(bf16), tr)

    w = _peer_u(e_sm, hnt, gates, _pack_table(peer_u))
    y = _peer_v(e_sm, w, x1t, _tile_rows(final_g.reshape(1, d))[0], _pack_table(peer_v))
    return _untile_rows(y).reshape(b, s, d)
```

```python
import functools

import jax
import jax.numpy as jnp
import numpy as np
from jax import lax
from jax.experimental import pallas as pl
from jax.experimental.pallas import tpu as pltpu

HEAD_DIM = 64
N_HEADS = 8
N_KV_HEADS = 2
GROUP = N_HEADS // N_KV_HEADS
ATTN_WIDTH = N_HEADS * HEAD_DIM
KV_WIDTH = N_KV_HEADS * HEAD_DIM
CONV_CH = 512
CONV_W = 31
CONV_PAD = (CONV_W - 1) // 2
GRID_W = 64
ROPE_THETA = 10000.0
ROPE_AXIS_DIM = HEAD_DIM // 2
N_KEYS = 128
PEER_HEADS = 8
PEER_DQ = 256
PEER_TOPK = 16
PEER_SEL = PEER_HEADS * PEER_TOPK
EPS = 1e-6

LANES = 128
SUBLANES = 8
HALO = 16
BF16_ROWS = 16
V_EXT = HEAD_DIM + BF16_ROWS
LOG2_E = 1.4426950408889634
VMEM_LIMIT = 56 * 1024 * 1024

ROW_TILE = 8
ROW_WORDS = ROW_TILE // 2
TILE_CHUNK = tuple((r % 2) * ROW_WORDS + r // 2 for r in range(ROW_TILE))

_NEG_INF = float("-inf")


def _cparams(sem, vmem=VMEM_LIMIT):
    return pltpu.CompilerParams(dimension_semantics=sem, vmem_limit_bytes=vmem)


def _group_mean_sq(v, ones_blockdiag):
    sq = v * v
    hi = sq.astype(jnp.bfloat16)
    lo = (sq - hi.astype(jnp.float32)).astype(jnp.bfloat16)
    tot = (jnp.dot(hi, ones_blockdiag, preferred_element_type=jnp.float32)
           + jnp.dot(lo, ones_blockdiag, preferred_element_type=jnp.float32))
    return tot * (1.0 / HEAD_DIM)


def _rope(v, cos, sin_signed, lane_in_pair_lo):
    width = v.shape[-1]
    up = pltpu.roll(v, width - ROPE_AXIS_DIM // 2, axis=1)
    down = pltpu.roll(v, ROPE_AXIS_DIM // 2, axis=1)
    partner = jnp.where(lane_in_pair_lo, up, down)
    return v * cos + partner * sin_signed


def _in_proj_kernel(x_ref, g1_ref, w_ref, qg_ref, kg_ref, cos_ref, sin_ref, ones_ref,
                    qt_ref, k_ref, vt_ref, hg_ref):
    x = x_ref[...]
    ms = jnp.mean(x * x, axis=-1, keepdims=True)
    h = (x * lax.rsqrt(ms + EPS) * g1_ref[...]).astype(jnp.bfloat16)
    p = jnp.dot(h, w_ref[...], preferred_element_type=jnp.float32)
    o1 = ATTN_WIDTH
    o2 = o1 + KV_WIDTH
    o3 = o2 + KV_WIDTH
    o4 = o3 + CONV_CH
    q = p[:, :o1]
    k = p[:, o1:o2]
    v = p[:, o2:o3]
    a = p[:, o3:o4]
    gate = p[:, o4:]

    ones_bd = ones_ref[...]
    qn = q * lax.rsqrt(_group_mean_sq(q, ones_bd) + EPS) * qg_ref[...]
    kn = k * lax.rsqrt(_group_mean_sq(k, ones_bd[:KV_WIDTH, :KV_WIDTH]) + EPS) * kg_ref[...]

    cos2 = cos_ref[...]
    sin2 = sin_ref[...]
    cos_q = jnp.concatenate([cos2] * (ATTN_WIDTH // LANES), axis=1)
    sin_q = jnp.concatenate([sin2] * (ATTN_WIDTH // LANES), axis=1)
    lane_q = lax.broadcasted_iota(jnp.int32, q.shape, 1)
    lane_k = lax.broadcasted_iota(jnp.int32, k.shape, 1)
    half = ROPE_AXIS_DIM // 2
    qr = _rope(qn, cos_q, sin_q, (lane_q % ROPE_AXIS_DIM) < half) * (HEAD_DIM ** -0.5 * LOG2_E)
    kr = _rope(kn, cos2, sin2, (lane_k % ROPE_AXIS_DIM) < half)

    qt = qr.T.astype(qt_ref.dtype)
    zeros = jnp.zeros((HEAD_DIM, qt.shape[1]), qt_ref.dtype)
    for hh in range(N_HEADS):
        g = hh // GROUP
        piece = qt[hh * HEAD_DIM:(hh + 1) * HEAD_DIM, :]
        qt_ref[0, hh, g * HEAD_DIM:(g + 1) * HEAD_DIM, :] = piece
        qt_ref[0, hh, (1 - g) * HEAD_DIM:(2 - g) * HEAD_DIM, :] = zeros
    k_ref[...] = kr.astype(k_ref.dtype)
    vt = v.T.astype(vt_ref.dtype)
    pad_row = lax.broadcasted_iota(jnp.int32, (V_EXT - HEAD_DIM, vt.shape[1]), 0)
    ones_pad = jnp.where(pad_row == 0, 1.0, 0.0).astype(vt_ref.dtype)
    for g in range(N_KV_HEADS):
        vt_ref[0, g, 0:HEAD_DIM, :] = vt[g * HEAD_DIM:(g + 1) * HEAD_DIM, :]
        vt_ref[0, g, HEAD_DIM:V_EXT, :] = ones_pad
    hg_ref[...] = a * (1.0 / (1.0 + jnp.exp(-gate)))


def _in_proj(x2, g1, w_in, qg, kg, cos_t, sin_t, ones_bd, batch, seq, tm):
    t, d = x2.shape
    n_s = seq // tm
    in_width = w_in.shape[1]
    return pl.pallas_call(
        _in_proj_kernel,
        grid=(t // tm,),
        in_specs=[
            pl.BlockSpec((tm, d), lambda i: (i, 0)),
            pl.BlockSpec((1, d), lambda i: (0, 0)),
            pl.BlockSpec((d, in_width), lambda i: (0, 0)),
            pl.BlockSpec((1, ATTN_WIDTH), lambda i: (0, 0)),
            pl.BlockSpec((1, KV_WIDTH), lambda i: (0, 0)),
            pl.BlockSpec((tm, LANES), lambda i: (i % n_s, 0)),
            pl.BlockSpec((tm, LANES), lambda i: (i % n_s, 0)),
            pl.BlockSpec((ATTN_WIDTH, ATTN_WIDTH), lambda i: (0, 0)),
        ],
        out_specs=[
            pl.BlockSpec((1, N_HEADS, KV_WIDTH, tm), lambda i: (i // n_s, 0, 0, i % n_s)),
            pl.BlockSpec((tm, KV_WIDTH), lambda i: (i, 0)),
            pl.BlockSpec((1, N_KV_HEADS, V_EXT, tm), lambda i: (i // n_s, 0, 0, i % n_s)),
            pl.BlockSpec((tm, CONV_CH), lambda i: (i, 0)),
        ],
        out_shape=[
            jax.ShapeDtypeStruct((batch, N_HEADS, KV_WIDTH, seq), jnp.bfloat16),
            jax.ShapeDtypeStruct((t, KV_WIDTH), jnp.bfloat16),
            jax.ShapeDtypeStruct((batch, N_KV_HEADS, V_EXT, seq), jnp.bfloat16),
            jax.ShapeDtypeStruct((t, CONV_CH), jnp.float32),
        ],
        compiler_params=_cparams(("parallel",)),
        name="in_proj",
    )(x2, g1, w_in, qg, kg, cos_t, sin_t, ones_bd)


SCORES_AHEAD = 2


def _attn_kernel(qt_ref, k_ref, vt_ref, o_ref, m_ref, acc_ref):
    ki = pl.program_id(2)

    @pl.when(ki == 0)
    def _():
        m_ref[...] = jnp.full(m_ref.shape, _NEG_INF, jnp.float32)
        acc_ref[...] = jnp.zeros(acc_ref.shape, jnp.float32)

    k = k_ref[...]

    def scores(h):
        return jnp.dot(k, qt_ref[0, h], preferred_element_type=jnp.float32)

    ahead = [scores(h) for h in range(SCORES_AHEAD)]
    for h in range(N_HEADS):
        g = h // GROUP
        s = ahead.pop(0)
        if h + SCORES_AHEAD < N_HEADS:
            ahead.append(scores(h + SCORES_AHEAD))
        m_prev = m_ref[h:h + 1, :]
        m_new = jnp.maximum(m_prev, jnp.max(s, axis=0, keepdims=True))
        alpha = jnp.exp2(m_prev - m_new)
        p = jnp.exp2(s - m_new).astype(jnp.bfloat16)
        pv = jnp.dot(vt_ref[0, g], p, preferred_element_type=jnp.float32)
        acc_ref[h] = alpha * acc_ref[h] + pv
        m_ref[h:h + 1, :] = m_new

    @pl.when(ki == pl.num_programs(2) - 1)
    def _():
        outs = []
        for h in range(N_HEADS):
            acc = acc_ref[h]
            outs.append(acc[:HEAD_DIM, :] / acc[HEAD_DIM:HEAD_DIM + 1, :])
        o_ref[...] = jnp.concatenate(outs, axis=0).T.astype(o_ref.dtype)


def _attention(qt, k, vt, batch, seq, tq, tk):
    t = k.shape[0]
    nq = seq // tq
    nk = seq // tk
    return pl.pallas_call(
        _attn_kernel,
        grid=(batch, nq, nk),
        in_specs=[
            pl.BlockSpec((1, N_HEADS, KV_WIDTH, tq), lambda b, qi, ki: (b, 0, 0, qi)),
            pl.BlockSpec((tk, KV_WIDTH), lambda b, qi, ki: (b * nk + ki, 0)),
            pl.BlockSpec((1, N_KV_HEADS, V_EXT, tk), lambda b, qi, ki: (b, 0, 0, ki)),
        ],
        out_specs=pl.BlockSpec((tq, ATTN_WIDTH), lambda b, qi, ki: (b * nq + qi, 0)),
        out_shape=jax.ShapeDtypeStruct((t, ATTN_WIDTH), jnp.bfloat16),
        scratch_shapes=[
            pltpu.VMEM((N_HEADS, tq), jnp.float32),
            pltpu.VMEM((N_HEADS, V_EXT, tq), jnp.float32),
        ],
        compiler_params=_cparams(("parallel", "parallel", "arbitrary")),
        name="attention",
    )(qt, k, vt)


CONV_ROWS = 32


def _mix_out_kernel(attn_ref, hg_ref, prev_ref, next_ref, cw_ref, cb_ref, lg_ref, lb_ref,
                    wo_ref, x_ref, g2_ref, x1t_ref, hn_ref, hnt_ref, ext_ref, conv_ref, *, n_s):
    tm = hg_ref.shape[0]
    si = pl.program_id(0) % n_s
    keep_prev = (si > 0).astype(jnp.float32)
    keep_next = (si < n_s - 1).astype(jnp.float32)
    ext_ref[0:HALO, :] = prev_ref[...] * keep_prev
    ext_ref[HALO:HALO + tm, :] = hg_ref[...]
    ext_ref[HALO + tm:HALO + tm + HALO, :] = next_ref[...] * keep_next

    cw = cw_ref[...]
    cb = cb_ref[...]
    lg = lg_ref[...]
    lb = lb_ref[...]

    def conv_step(c, carry):
        start = pl.multiple_of(c * CONV_ROWS, CONV_ROWS)
        rows = CONV_ROWS + 2 * HALO
        parts = []
        for lt in range(CONV_CH // LANES):
            cols = slice(lt * LANES, (lt + 1) * LANES)
            win = ext_ref[pl.ds(start, rows), cols]
            acc_l = jnp.zeros((CONV_ROWS, LANES), jnp.float32)
            for shift in range(SUBLANES):
                shifted = win if shift == 0 else pltpu.roll(win, rows - shift, axis=0)
                for j in range(CONV_W):
                    off = HALO - CONV_PAD + j
                    if off % SUBLANES == shift:
                        base = off - shift
                        acc_l = acc_l + shifted[base:base + CONV_ROWS, :] * cw[j:j + 1, cols]
            parts.append(acc_l)
        acc = jnp.concatenate(parts, axis=1)
        acc = acc + cb
        mu = jnp.mean(acc, axis=-1, keepdims=True)
        cen = acc - mu
        var = jnp.mean(cen * cen, axis=-1, keepdims=True)
        y = cen * lax.rsqrt(var + EPS) * lg + lb
        y = y * (1.0 / (1.0 + jnp.exp(-y)))
        conv_ref[pl.ds(start, CONV_ROWS), :] = y.astype(conv_ref.dtype)
        return carry

    lax.fori_loop(0, tm // CONV_ROWS, conv_step, 0)

    mixed = (jnp.dot(attn_ref[...], wo_ref[0:ATTN_WIDTH, :], preferred_element_type=jnp.float32)
             + jnp.dot(conv_ref[...], wo_ref[ATTN_WIDTH:, :], preferred_element_type=jnp.float32))
    x1 = x_ref[...] + mixed
    ms = jnp.mean(x1 * x1, axis=-1, keepdims=True)
    hn = x1 * lax.rsqrt(ms + EPS) * g2_ref[...]
    hn_ref[...] = hn
    for r, c in enumerate(TILE_CHUNK):
        x1t_ref[:, r, :] = x1[:, c * LANES:(c + 1) * LANES]
        hnt_ref[:, r, :] = hn[:, c * LANES:(c + 1) * LANES]


def _mix_out(attn, hg, cw, cb, lg, lb, w_out, x2, g2, seq, tm):
    t, d = x2.shape
    n_s = seq // tm
    hb = tm // HALO
    last_halo = t // HALO - 1
    return pl.pallas_call(
        functools.partial(_mix_out_kernel, n_s=n_s),
        grid=(t // tm,),
        in_specs=[
            pl.BlockSpec((tm, ATTN_WIDTH), lambda i: (i, 0)),
            pl.BlockSpec((tm, CONV_CH), lambda i: (i, 0)),
            pl.BlockSpec((HALO, CONV_CH), lambda i: (jnp.maximum(i * hb - 1, 0), 0)),
            pl.BlockSpec((HALO, CONV_CH), lambda i: (jnp.minimum((i + 1) * hb, last_halo), 0)),
            pl.BlockSpec((CONV_W + 1, CONV_CH), lambda i: (0, 0)),
            pl.BlockSpec((1, CONV_CH), lambda i: (0, 0)),
            pl.BlockSpec((1, CONV_CH), lambda i: (0, 0)),
            pl.BlockSpec((1, CONV_CH), lambda i: (0, 0)),
            pl.BlockSpec((d, d), lambda i: (0, 0)),
            pl.BlockSpec((tm, d), lambda i: (i, 0)),
            pl.BlockSpec((1, d), lambda i: (0, 0)),
        ],
        out_specs=[
            pl.BlockSpec((tm, ROW_TILE, LANES), lambda i: (i, 0, 0)),
            pl.BlockSpec((tm, d), lambda i: (i, 0)),
            pl.BlockSpec((tm, ROW_TILE, LANES), lambda i: (i, 0, 0)),
        ],
        out_shape=[
            jax.ShapeDtypeStruct((t, ROW_TILE, LANES), jnp.float32),
            jax.ShapeDtypeStruct((t, d), jnp.float32),
            jax.ShapeDtypeStruct((t, ROW_TILE, LANES), jnp.float32),
        ],
        scratch_shapes=[
            pltpu.VMEM((tm + 2 * HALO, CONV_CH), jnp.float32),
            pltpu.VMEM((tm, CONV_CH), jnp.bfloat16),
        ],
        compiler_params=_cparams(("parallel",)),
        name="mix_out",
    )(attn, hg, hg, hg, cw, cb, lg, lb, w_out, x2, g2)


def _top_rows(s, order, payload, count):
    big = jnp.float32(2 ** 30)
    vals = []
    picks = []
    for _ in range(count):
        m = jnp.max(s, axis=0, keepdims=True)
        first = jnp.min(jnp.where(s == m, order, big), axis=0, keepdims=True)
        chosen = order == first
        if payload is order:
            pick = first
        else:
            pick = jnp.max(jnp.where(chosen, payload, -1.0), axis=0, keepdims=True)
        vals.append(m)
        picks.append(pick)
        s = jnp.where(chosen, _NEG_INF, s)
    return jnp.concatenate(vals, axis=0), jnp.concatenate(picks, axis=0)


def _candidate_blocks():
    blocks = [(0, 0, 16)]
    for a in range(1, 8):
        blocks.append((a, 0, 8))
    return blocks


def _route_kernel(hn_ref, wqt_ref, keys_ref, e_ref, g_ref, words_ref):
    hn = hn_ref[...].astype(jnp.bfloat16)
    qpt = lax.dot_general(wqt_ref[...], hn, (((1,), (1,)), ((), ())),
                          preferred_element_type=jnp.float32)
    qpt = qpt.astype(jnp.bfloat16)
    tm = hn.shape[0]
    half = PEER_DQ // 2
    words_ref[PEER_SEL // 2:, :] = jnp.zeros((LANES - PEER_SEL // 2, tm), jnp.int32)
    k0 = keys_ref[0]
    k1 = keys_ref[1]
    row_iota = lax.broadcasted_iota(jnp.int32, (N_KEYS, LANES), 0).astype(jnp.float32)

    blocks = _candidate_blocks()
    flat_parts = []
    for a, b0, nb in blocks:
        flat_parts.append(a * PEER_TOPK + b0 + lax.broadcasted_iota(jnp.int32, (nb, LANES), 0))
    tail_flat = (8 + lax.broadcasted_iota(jnp.int32, (8, LANES), 0)) * PEER_TOPK
    flat = jnp.concatenate(flat_parts + [tail_flat], axis=0).astype(jnp.float32)

    for h in range(PEER_HEADS):
        q1 = qpt[h * PEER_DQ:h * PEER_DQ + half, :]
        q2 = qpt[h * PEER_DQ + half:(h + 1) * PEER_DQ, :]
        s1 = jnp.dot(k0, q1, preferred_element_type=jnp.float32)
        s2 = jnp.dot(k1, q2, preferred_element_type=jnp.float32)
        for c in range(tm // LANES):
            sl = slice(c * LANES, (c + 1) * LANES)
            v1, i1 = _top_rows(s1[:, sl], row_iota, row_iota, PEER_TOPK)
            v2, i2 = _top_rows(s2[:, sl], row_iota, row_iota, PEER_TOPK)
            cs = []
            ce = []
            for a, b0, nb in blocks:
                cs.append(v1[a:a + 1, :] + v2[b0:b0 + nb, :])
                ce.append(i1[a:a + 1, :] * N_KEYS + i2[b0:b0 + nb, :])
            cs.append(v1[8:16, :] + v2[0:1, :])
            ce.append(i1[8:16, :] * N_KEYS + i2[0:1, :])
            cand_s = jnp.concatenate(cs, axis=0)
            cand_e = jnp.concatenate(ce, axis=0)
            top_s, top_e = _top_rows(cand_s, flat, cand_e, PEER_TOPK)
            ex = jnp.exp(top_s - top_s[0:1, :])
            gates = ex / jnp.sum(ex, axis=0, keepdims=True)
            off = top_e.astype(jnp.int32) * ROW_WORDS
            half_k = PEER_TOPK // 2
            words_ref[h * half_k:(h + 1) * half_k, sl] = (off[half_k:, :] << 16) | off[:half_k, :]
            g_ref[h, :, sl] = gates
    e_ref[...] = words_ref[...].T


def _route(hn, wqt, keys, tm):
    t, d = hn.shape
    return pl.pallas_call(
        _route_kernel,
        grid=(t // tm,),
        in_specs=[
            pl.BlockSpec((tm, d), lambda i: (i, 0)),
            pl.BlockSpec(wqt.shape, lambda i: (0, 0)),
            pl.BlockSpec(keys.shape, lambda i: (0, 0, 0)),
        ],
        out_specs=[
            pl.BlockSpec((tm, LANES), lambda i: (i, 0)),
            pl.BlockSpec((PEER_HEADS, PEER_TOPK, tm), lambda i: (0, 0, i)),
        ],
        out_shape=[
            jax.ShapeDtypeStruct((t, LANES), jnp.int32),
            jax.ShapeDtypeStruct((PEER_HEADS, PEER_TOPK, t), jnp.float32),
        ],
        scratch_shapes=[pltpu.VMEM((LANES, tm), jnp.int32)],
        compiler_params=_cparams(("parallel",)),
        name="route",
    )(hn, wqt, keys)


PEER_TOK = 128


def _expert_row(tbl_ref, off):
    words = tbl_ref[pl.ds(pl.multiple_of(off, ROW_WORDS), ROW_WORDS), :]
    return pltpu.bitcast(words, jnp.bfloat16).astype(jnp.float32)


def _expert_offsets(idx_row, head):
    lo, hi = [], []
    for j in range(PEER_TOPK // 2):
        word = idx_row[head * (PEER_TOPK // 2) + j]
        lo.append(word & 0xFFFF)
        hi.append(lax.shift_right_logical(word, 16))
    return lo + hi


def _gelu_tanh(x):
    c = np.float32(np.sqrt(2.0 / np.pi))
    return 0.5 * x * (1.0 + jnp.tanh(c * (x + 0.044715 * (x * x * x))))


def _fold_rows(rows, x, x_swapped, masks):
    lo4, mod4_lo2, even = masks
    a, e, c, g, b, f, d, h = rows

    def level1(p, q):
        t1 = jnp.where(lo4, p, q)
        t2 = pltpu.roll(jnp.where(lo4, q, p), 4, axis=0)
        return t1 * x + t2 * x_swapped

    def level2(u, v):
        uu = u + pltpu.roll(u, 6, axis=0)
        vv = v + pltpu.roll(v, 2, axis=0)
        return jnp.where(mod4_lo2, uu, vv)

    def level3(u, v):
        uu = u + pltpu.roll(u, 7, axis=0)
        vv = v + pltpu.roll(v, 1, axis=0)
        return jnp.where(even, uu, vv)

    return level3(level2(level1(a, b), level1(c, d)), level2(level1(e, f), level1(g, h)))


U_TOKENS_PER_STEP = 8


def _peer_u_kernel(e_ref, x_ref, gate_ref, tbl_ref, w_ref, s_ref, part_a, part_b):
    lane = lax.broadcasted_iota(jnp.int32, (PEER_SEL, PEER_TOK), 1)
    sub = lax.broadcasted_iota(jnp.int32, (SUBLANES, LANES), 0)
    masks = (sub < 4, (sub % 4) < 2, (sub % 2) == 0)

    def fold(t, part_ref):
        e_row = e_ref.at[t]
        x = x_ref[t]
        x_swapped = pltpu.roll(x, 4, axis=0)
        for head in range(PEER_HEADS):
            offs = _expert_offsets(e_row, head)
            for half in range(PEER_TOPK // SUBLANES):
                grp = head * (PEER_TOPK // SUBLANES) + half
                rows = [_expert_row(tbl_ref, offs[half * SUBLANES + j]) for j in range(SUBLANES)]
                part_ref[grp * SUBLANES:(grp + 1) * SUBLANES, :] = _fold_rows(rows, x, x_swapped, masks)

    def finish(t, part_ref):
        s = jnp.sum(part_ref[...], axis=1, keepdims=True)
        s_ref[...] = jnp.where(lane == t, s, s_ref[...])

    s_ref[...] = jnp.zeros(s_ref.shape, jnp.float32)
    part_b[...] = jnp.zeros(part_b.shape, jnp.float32)

    def token_block(i, carry):
        t0 = U_TOKENS_PER_STEP * i
        for j in range(U_TOKENS_PER_STEP):
            cur, prev = (part_a, part_b) if j % 2 == 0 else (part_b, part_a)
            fold(t0 + j, cur)
            finish(t0 + j - 1, prev)
        return carry

    lax.fori_loop(0, PEER_TOK // U_TOKENS_PER_STEP, token_block, 0)
    finish(PEER_TOK - 1, part_b)
    gates = gate_ref[...].reshape(PEER_SEL, PEER_TOK)
    w_ref[...] = gates * _gelu_tanh(s_ref[...])


def _peer_u(e_sm, x3, gates, tbl):
    t = x3.shape[0]
    return pl.pallas_call(
        _peer_u_kernel,
        grid=(t // PEER_TOK,),
        in_specs=[
            pl.BlockSpec((PEER_TOK, LANES), lambda i: (i, 0), memory_space=pltpu.SMEM),
            pl.BlockSpec((PEER_TOK, SUBLANES, LANES), lambda i: (i, 0, 0)),
            pl.BlockSpec((PEER_HEADS, PEER_TOPK, PEER_TOK), lambda i: (0, 0, i)),
            pl.BlockSpec(tbl.shape, lambda i: (0, 0), pipeline_mode=pl.Buffered(1)),
        ],
        out_specs=pl.BlockSpec((PEER_SEL, PEER_TOK), lambda i: (0, i)),
        out_shape=jax.ShapeDtypeStruct((PEER_SEL, t), jnp.float32),
        scratch_shapes=[
            pltpu.VMEM((PEER_SEL, PEER_TOK), jnp.float32),
            pltpu.VMEM((PEER_SEL, LANES), jnp.float32),
            pltpu.VMEM((PEER_SEL, LANES), jnp.float32),
        ],
        compiler_params=_cparams(("arbitrary",)),
        name="peer_u",
    )(e_sm, x3, gates, tbl)


def _peer_v_kernel(e_ref, wt_ref, x1_ref, fg_ref, tbl_ref, y_ref):
    fg = fg_ref[...]
    d_model = SUBLANES * LANES
    lane = lax.broadcasted_iota(jnp.int32, (PEER_SEL, PEER_TOK), 1)

    def spread(t):
        col = jnp.sum(jnp.where(lane == t, wt_ref[...], 0.0), axis=1, keepdims=True)
        return jnp.broadcast_to(col, (PEER_SEL, LANES))

    def token(t, wb):
        wb_next = spread(jnp.minimum(t + 1, PEER_TOK - 1))
        e_row = e_ref.at[t]
        acc = x1_ref[t]
        for head in range(PEER_HEADS):
            offs = _expert_offsets(e_row, head)
            for half in range(PEER_TOPK // SUBLANES):
                terms = []
                for j in range(SUBLANES):
                    slot = half * SUBLANES + j
                    k = head * PEER_TOPK + slot
                    terms.append(wb[k:k + 1, :] * _expert_row(tbl_ref, offs[slot]))
                while len(terms) > 1:
                    terms = [terms[i] + terms[i + 1] for i in range(0, len(terms), 2)]
                acc = acc + terms[0]
        y_ref[t] = acc
        return wb_next

    lax.fori_loop(0, PEER_TOK, token, spread(0))
    x2 = y_ref[...]
    ms = jnp.sum(x2 * x2, axis=(1, 2), keepdims=True) * (1.0 / d_model)
    y_ref[...] = x2 * lax.rsqrt(ms + EPS) * fg


def _peer_v(e_sm, wt, x13, fg, tbl):
    t = x13.shape[0]
    return pl.pallas_call(
        _peer_v_kernel,
        grid=(t // PEER_TOK,),
        in_specs=[
            pl.BlockSpec((PEER_TOK, LANES), lambda i: (i, 0), memory_space=pltpu.SMEM),
            pl.BlockSpec((PEER_SEL, PEER_TOK), lambda i: (0, i)),
            pl.BlockSpec((PEER_TOK, SUBLANES, LANES), lambda i: (i, 0, 0)),
            pl.BlockSpec((SUBLANES, LANES), lambda i: (0, 0)),
            pl.BlockSpec(tbl.shape, lambda i: (0, 0), pipeline_mode=pl.Buffered(1)),
        ],
        out_specs=pl.BlockSpec((PEER_TOK, SUBLANES, LANES), lambda i: (i, 0, 0)),
        out_shape=jax.ShapeDtypeStruct((t, SUBLANES, LANES), jnp.float32),
        compiler_params=_cparams(("arbitrary",)),
        name="peer_v",
    )(e_sm, wt, x13, fg, tbl)


def _pack_table(tbl):
    n, d = tbl.shape
    assert d == ROW_TILE * LANES
    tb = tbl.astype(jnp.bfloat16)
    pairs = jnp.stack([tb[:, :d // 2], tb[:, d // 2:]], axis=-1)
    return lax.bitcast_convert_type(pairs, jnp.int32).reshape(n * ROW_WORDS, LANES)


def _tile_rows(a):
    t, d = a.shape
    return a.reshape(t, 2, ROW_WORDS, LANES).transpose(0, 2, 1, 3).reshape(t, ROW_TILE, LANES)


def _untile_rows(a):
    t = a.shape[0]
    return a.reshape(t, ROW_WORDS, 2, LANES).transpose(0, 2, 1, 3).reshape(t, ROW_TILE * LANES)


def _rope_tables(seq):
    pos = jnp.arange(seq)
    row = (pos // GRID_W).astype(jnp.float32)
    col = (pos % GRID_W).astype(jnp.float32)
    n_pairs = ROPE_AXIS_DIM // 2
    inv_freq = ROPE_THETA ** (-jnp.arange(n_pairs, dtype=jnp.float32) / n_pairs)
    ang_r = row[:, None] * inv_freq[None, :]
    ang_c = col[:, None] * inv_freq[None, :]
    cos = jnp.concatenate([jnp.cos(ang_r), jnp.cos(ang_r), jnp.cos(ang_c), jnp.cos(ang_c)], axis=1)
    sin = jnp.concatenate([-jnp.sin(ang_r), jnp.sin(ang_r), -jnp.sin(ang_c), jnp.sin(ang_c)], axis=1)
    reps = LANES // HEAD_DIM
    return jnp.tile(cos, (1, reps)), jnp.tile(sin, (1, reps))


def kernel(x, norm1_g, w_in, q_norm_g, k_norm_g, conv_dw, conv_b, conv_ln_g, conv_ln_b, w_out,
           norm2_g, peer_wq, peer_keys, peer_u, peer_v, final_g):
    b, s, d = x.shape
    t = b * s
    assert d == SUBLANES * LANES and s % GRID_W == 0 and N_KV_HEADS == 2
    tm = min(512, s)
    tq = min(512, s)
    tk = min(1024, s)
    tr = min(256, s)
    assert s % tm == 0 and t % PEER_TOK == 0

    f32 = jnp.float32
    bf16 = jnp.bfloat16
    x2 = x.reshape(t, d)
    cos_t, sin_t = _rope_tables(s)
    head_id = jnp.arange(ATTN_WIDTH) // HEAD_DIM
    ones_bd = (head_id[:, None] == head_id[None, :]).astype(bf16)

    qt, k, vt, hg = _in_proj(
        x2, norm1_g.reshape(1, d), w_in.astype(bf16),
        jnp.tile(q_norm_g, N_HEADS).reshape(1, ATTN_WIDTH),
        jnp.tile(k_norm_g, N_KV_HEADS).reshape(1, KV_WIDTH),
        cos_t, sin_t, ones_bd, b, s, tm)

    attn = _attention(qt, k, vt, b, s, tq, tk)

    cw = jnp.concatenate([conv_dw.reshape(CONV_W, CONV_CH), jnp.zeros((1, CONV_CH), f32)], axis=0)
    x1t, hn, hnt = _mix_out(attn, hg, cw, conv_b.reshape(1, CONV_CH), conv_ln_g.reshape(1, CONV_CH),
                      conv_ln_b.reshape(1, CONV_CH), w_out.astype(bf16), x2,
                      norm2_g.reshape(1, d), s, tm)

    e_sm, gates = _route(hn, peer_wq.T.astype(bf16), peer_keys.astype(bf16), tr)

    w = _peer_u(e_sm, hnt, gates, _pack_table(peer_u))
    y = _peer_v(e_sm, w, x1t, _tile_rows(final_g.reshape(1, d))[0], _pack_table(peer_v))
    return _untile_rows(y).reshape(b, s, d)
```

```python
import functools

import jax
import jax.numpy as jnp
import numpy as np
from jax import lax
from jax.experimental import pallas as pl
from jax.experimental.pallas import tpu as pltpu

HEAD_DIM = 64
N_HEADS = 8
N_KV_HEADS = 2
GROUP = N_HEADS // N_KV_HEADS
ATTN_WIDTH = N_HEADS * HEAD_DIM
KV_WIDTH = N_KV_HEADS * HEAD_DIM
CONV_CH = 512
CONV_W = 31
CONV_PAD = (CONV_W - 1) // 2
GRID_W = 64
ROPE_THETA = 10000.0
ROPE_AXIS_DIM = HEAD_DIM // 2
N_KEYS = 128
PEER_HEADS = 8
PEER_DQ = 256
PEER_TOPK = 16
PEER_SEL = PEER_HEADS * PEER_TOPK
EPS = 1e-6

LANES = 128
SUBLANES = 8
HALO = 16
BF16_ROWS = 16
V_EXT = HEAD_DIM + BF16_ROWS
LOG2_E = 1.4426950408889634
VMEM_LIMIT = 56 * 1024 * 1024

ROW_TILE = 8
ROW_WORDS = ROW_TILE // 2
TILE_CHUNK = tuple((r % 2) * ROW_WORDS + r // 2 for r in range(ROW_TILE))

_NEG_INF = float("-inf")


def _cparams(sem, vmem=VMEM_LIMIT):
    return pltpu.CompilerParams(dimension_semantics=sem, vmem_limit_bytes=vmem)


def _group_mean_sq(v, ones_blockdiag):
    sq = v * v
    hi = sq.astype(jnp.bfloat16)
    lo = (sq - hi.astype(jnp.float32)).astype(jnp.bfloat16)
    tot = (jnp.dot(hi, ones_blockdiag, preferred_element_type=jnp.float32)
           + jnp.dot(lo, ones_blockdiag, preferred_element_type=jnp.float32))
    return tot * (1.0 / HEAD_DIM)


def _rope(v, cos, sin_signed, lane_in_pair_lo):
    width = v.shape[-1]
    up = pltpu.roll(v, width - ROPE_AXIS_DIM // 2, axis=1)
    down = pltpu.roll(v, ROPE_AXIS_DIM // 2, axis=1)
    partner = jnp.where(lane_in_pair_lo, up, down)
    return v * cos + partner * sin_signed


def _in_proj_kernel(x_ref, g1_ref, w_ref, qg_ref, kg_ref, cos_ref, sin_ref, ones_ref,
                    qt_ref, k_ref, vt_ref, hg_ref):
    x = x_ref[...]
    ms = jnp.mean(x * x, axis=-1, keepdims=True)
    h = (x * lax.rsqrt(ms + EPS) * g1_ref[...]).astype(jnp.bfloat16)
    p = jnp.dot(h, w_ref[...], preferred_element_type=jnp.float32)
    o1 = ATTN_WIDTH
    o2 = o1 + KV_WIDTH
    o3 = o2 + KV_WIDTH
    o4 = o3 + CONV_CH
    q = p[:, :o1]
    k = p[:, o1:o2]
    v = p[:, o2:o3]
    a = p[:, o3:o4]
    gate = p[:, o4:]

    ones_bd = ones_ref[...]
    qn = q * lax.rsqrt(_group_mean_sq(q, ones_bd) + EPS) * qg_ref[...]
    kn = k * lax.rsqrt(_group_mean_sq(k, ones_bd[:KV_WIDTH, :KV_WIDTH]) + EPS) * kg_ref[...]

    cos2 = cos_ref[...]
    sin2 = sin_ref[...]
    cos_q = jnp.concatenate([cos2] * (ATTN_WIDTH // LANES), axis=1)
    sin_q = jnp.concatenate([sin2] * (ATTN_WIDTH // LANES), axis=1)
    lane_q = lax.broadcasted_iota(jnp.int32, q.shape, 1)
    lane_k = lax.broadcasted_iota(jnp.int32, k.shape, 1)
    half = ROPE_AXIS_DIM // 2
    qr = _rope(qn, cos_q, sin_q, (lane_q % ROPE_AXIS_DIM) < half) * (HEAD_DIM ** -0.5 * LOG2_E)
    kr = _rope(kn, cos2, sin2, (lane_k % ROPE_AXIS_DIM) < half)

    qt = qr.T.astype(qt_ref.dtype)
    zeros = jnp.zeros((HEAD_DIM, qt.shape[1]), qt_ref.dtype)
    for hh in range(N_HEADS):
        g = hh // GROUP
        piece = qt[hh * HEAD_DIM:(hh + 1) * HEAD_DIM, :]
        qt_ref[0, hh, g * HEAD_DIM:(g + 1) * HEAD_DIM, :] = piece
        qt_ref[0, hh, (1 - g) * HEAD_DIM:(2 - g) * HEAD_DIM, :] = zeros
    k_ref[...] = kr.astype(k_ref.dtype)
    vt = v.T.astype(vt_ref.dtype)
    pad_row = lax.broadcasted_iota(jnp.int32, (V_EXT - HEAD_DIM, vt.shape[1]), 0)
    ones_pad = jnp.where(pad_row == 0, 1.0, 0.0).astype(vt_ref.dtype)
    for g in range(N_KV_HEADS):
        vt_ref[0, g, 0:HEAD_DIM, :] = vt[g * HEAD_DIM:(g + 1) * HEAD_DIM, :]
        vt_ref[0, g, HEAD_DIM:V_EXT, :] = ones_pad
    hg_ref[...] = a * (1.0 / (1.0 + jnp.exp(-gate)))


def _in_proj(x2, g1, w_in, qg, kg, cos_t, sin_t, ones_bd, batch, seq, tm):
    t, d = x2.shape
    n_s = seq // tm
    in_width = w_in.shape[1]
    return pl.pallas_call(
        _in_proj_kernel,
        grid=(t // tm,),
        in_specs=[
            pl.BlockSpec((tm, d), lambda i: (i, 0)),
            pl.BlockSpec((1, d), lambda i: (0, 0)),
            pl.BlockSpec((d, in_width), lambda i: (0, 0)),
            pl.BlockSpec((1, ATTN_WIDTH), lambda i: (0, 0)),
            pl.BlockSpec((1, KV_WIDTH), lambda i: (0, 0)),
            pl.BlockSpec((tm, LANES), lambda i: (i % n_s, 0)),
            pl.BlockSpec((tm, LANES), lambda i: (i % n_s, 0)),
            pl.BlockSpec((ATTN_WIDTH, ATTN_WIDTH), lambda i: (0, 0)),
        ],
        out_specs=[
            pl.BlockSpec((1, N_HEADS, KV_WIDTH, tm), lambda i: (i // n_s, 0, 0, i % n_s)),
            pl.BlockSpec((tm, KV_WIDTH), lambda i: (i, 0)),
            pl.BlockSpec((1, N_KV_HEADS, V_EXT, tm), lambda i: (i // n_s, 0, 0, i % n_s)),
            pl.BlockSpec((tm, CONV_CH), lambda i: (i, 0)),
        ],
        out_shape=[
            jax.ShapeDtypeStruct((batch, N_HEADS, KV_WIDTH, seq), jnp.bfloat16),
            jax.ShapeDtypeStruct((t, KV_WIDTH), jnp.bfloat16),
            jax.ShapeDtypeStruct((batch, N_KV_HEADS, V_EXT, seq), jnp.bfloat16),
            jax.ShapeDtypeStruct((t, CONV_CH), jnp.float32),
        ],
        compiler_params=_cparams(("parallel",)),
        name="in_proj",
    )(x2, g1, w_in, qg, kg, cos_t, sin_t, ones_bd)


SCORES_AHEAD = 2


def _attn_kernel(qt_ref, k_ref, vt_ref, o_ref, m_ref, acc_ref):
    ki = pl.program_id(2)

    @pl.when(ki == 0)
    def _():
        m_ref[...] = jnp.full(m_ref.shape, _NEG_INF, jnp.float32)
        acc_ref[...] = jnp.zeros(acc_ref.shape, jnp.float32)

    k = k_ref[...]

    def scores(h):
        return jnp.dot(k, qt_ref[0, h], preferred_element_type=jnp.float32)

    ahead = [scores(h) for h in range(SCORES_AHEAD)]
    for h in range(N_HEADS):
        g = h // GROUP
        s = ahead.pop(0)
        if h + SCORES_AHEAD < N_HEADS:
            ahead.append(scores(h + SCORES_AHEAD))
        m_prev = m_ref[h:h + 1, :]
        m_new = jnp.maximum(m_prev, jnp.max(s, axis=0, keepdims=True))
        alpha = jnp.exp2(m_prev - m_new)
        p = jnp.exp2(s - m_new).astype(jnp.bfloat16)
        pv = jnp.dot(vt_ref[0, g], p, preferred_element_type=jnp.float32)
        acc_ref[h] = alpha * acc_ref[h] + pv
        m_ref[h:h + 1, :] = m_new

    @pl.when(ki == pl.num_programs(2) - 1)
    def _():
        outs = []
        for h in range(N_HEADS):
            acc = acc_ref[h]
            outs.append(acc[:HEAD_DIM, :] / acc[HEAD_DIM:HEAD_DIM + 1, :])
        o_ref[...] = jnp.concatenate(outs, axis=0).T.astype(o_ref.dtype)


def _attention(qt, k, vt, batch, seq, tq, tk):
    t = k.shape[0]
    nq = seq // tq
    nk = seq // tk
    return pl.pallas_call(
        _attn_kernel,
        grid=(batch, nq, nk),
        in_specs=[
            pl.BlockSpec((1, N_HEADS, KV_WIDTH, tq), lambda b, qi, ki: (b, 0, 0, qi)),
            pl.BlockSpec((tk, KV_WIDTH), lambda b, qi, ki: (b * nk + ki, 0)),
            pl.BlockSpec((1, N_KV_HEADS, V_EXT, tk), lambda b, qi, ki: (b, 0, 0, ki)),
        ],
        out_specs=pl.BlockSpec((tq, ATTN_WIDTH), lambda b, qi, ki: (b * nq + qi, 0)),
        out_shape=jax.ShapeDtypeStruct((t, ATTN_WIDTH), jnp.bfloat16),
        scratch_shapes=[
            pltpu.VMEM((N_HEADS, tq), jnp.float32),
            pltpu.VMEM((N_HEADS, V_EXT, tq), jnp.float32),
        ],
        compiler_params=_cparams(("parallel", "parallel", "arbitrary")),
        name="attention",
    )(qt, k, vt)


CONV_ROWS = 32


def _mix_out_kernel(attn_ref, hg_ref, prev_ref, next_ref, cw_ref, cb_ref, lg_ref, lb_ref,
                    wo_ref, x_ref, g2_ref, x1_ref, hn_ref, hnt_ref, ext_ref, conv_ref, *, n_s):
    tm = hg_ref.shape[0]
    si = pl.program_id(0) % n_s
    keep_prev = (si > 0).astype(jnp.float32)
    keep_next = (si < n_s - 1).astype(jnp.float32)
    ext_ref[0:HALO, :] = prev_ref[...] * keep_prev
    ext_ref[HALO:HALO + tm, :] = hg_ref[...]
    ext_ref[HALO + tm:HALO + tm + HALO, :] = next_ref[...] * keep_next

    cw = cw_ref[...]
    cb = cb_ref[...]
    lg = lg_ref[...]
    lb = lb_ref[...]

    def conv_step(c, carry):
        start = pl.multiple_of(c * CONV_ROWS, CONV_ROWS)
        rows = CONV_ROWS + 2 * HALO
        parts = []
        for lt in range(CONV_CH // LANES):
            cols = slice(lt * LANES, (lt + 1) * LANES)
            win = ext_ref[pl.ds(start, rows), cols]
            acc_l = jnp.zeros((CONV_ROWS, LANES), jnp.float32)
            for shift in range(SUBLANES):
                shifted = win if shift == 0 else pltpu.roll(win, rows - shift, axis=0)
                for j in range(CONV_W):
                    off = HALO - CONV_PAD + j
                    if off % SUBLANES == shift:
                        base = off - shift
                        acc_l = acc_l + shifted[base:base + CONV_ROWS, :] * cw[j:j + 1, cols]
            parts.append(acc_l)
        acc = jnp.concatenate(parts, axis=1)
        acc = acc + cb
        mu = jnp.mean(acc, axis=-1, keepdims=True)
        cen = acc - mu
        var = jnp.mean(cen * cen, axis=-1, keepdims=True)
        y = cen * lax.rsqrt(var + EPS) * lg + lb
        y = y * (1.0 / (1.0 + jnp.exp(-y)))
        conv_ref[pl.ds(start, CONV_ROWS), :] = y.astype(conv_ref.dtype)
        return carry

    lax.fori_loop(0, tm // CONV_ROWS, conv_step, 0)

    mixed = (jnp.dot(attn_ref[...], wo_ref[0:ATTN_WIDTH, :], preferred_element_type=jnp.float32)
             + jnp.dot(conv_ref[...], wo_ref[ATTN_WIDTH:, :], preferred_element_type=jnp.float32))
    x1 = x_ref[...] + mixed
    x1_ref[...] = x1
    ms = jnp.mean(x1 * x1, axis=-1, keepdims=True)
    hn = x1 * lax.rsqrt(ms + EPS) * g2_ref[...]
    hn_ref[...] = hn
    for r, c in enumerate(TILE_CHUNK):
        hnt_ref[:, r, :] = hn[:, c * LANES:(c + 1) * LANES]


def _mix_out(attn, hg, cw, cb, lg, lb, w_out, x2, g2, seq, tm):
    t, d = x2.shape
    n_s = seq // tm
    hb = tm // HALO
    last_halo = t // HALO - 1
    return pl.pallas_call(
        functools.partial(_mix_out_kernel, n_s=n_s),
        grid=(t // tm,),
        in_specs=[
            pl.BlockSpec((tm, ATTN_WIDTH), lambda i: (i, 0)),
            pl.BlockSpec((tm, CONV_CH), lambda i: (i, 0)),
            pl.BlockSpec((HALO, CONV_CH), lambda i: (jnp.maximum(i * hb - 1, 0), 0)),
            pl.BlockSpec((HALO, CONV_CH), lambda i: (jnp.minimum((i + 1) * hb, last_halo), 0)),
            pl.BlockSpec((CONV_W + 1, CONV_CH), lambda i: (0, 0)),
            pl.BlockSpec((1, CONV_CH), lambda i: (0, 0)),
            pl.BlockSpec((1, CONV_CH), lambda i: (0, 0)),
            pl.BlockSpec((1, CONV_CH), lambda i: (0, 0)),
            pl.BlockSpec((d, d), lambda i: (0, 0)),
            pl.BlockSpec((tm, d), lambda i: (i, 0)),
            pl.BlockSpec((1, d), lambda i: (0, 0)),
        ],
        out_specs=[
            pl.BlockSpec((tm, d), lambda i: (i, 0)),
            pl.BlockSpec((tm, d), lambda i: (i, 0)),
            pl.BlockSpec((tm, ROW_TILE, LANES), lambda i: (i, 0, 0)),
        ],
        out_shape=[
            jax.ShapeDtypeStruct((t, d), jnp.float32),
            jax.ShapeDtypeStruct((t, d), jnp.float32),
            jax.ShapeDtypeStruct((t, ROW_TILE, LANES), jnp.float32),
        ],
        scratch_shapes=[
            pltpu.VMEM((tm + 2 * HALO, CONV_CH), jnp.float32),
            pltpu.VMEM((tm, CONV_CH), jnp.bfloat16),
        ],
        compiler_params=_cparams(("parallel",)),
        name="mix_out",
    )(attn, hg, hg, hg, cw, cb, lg, lb, w_out, x2, g2)


def _top_rows(s, order, payload, count):
    big = jnp.float32(2 ** 30)
    vals = []
    picks = []
    for _ in range(count):
        m = jnp.max(s, axis=0, keepdims=True)
        first = jnp.min(jnp.where(s == m, order, big), axis=0, keepdims=True)
        chosen = order == first
        if payload is order:
            pick = first
        else:
            pick = jnp.max(jnp.where(chosen, payload, -1.0), axis=0, keepdims=True)
        vals.append(m)
        picks.append(pick)
        s = jnp.where(chosen, _NEG_INF, s)
    return jnp.concatenate(vals, axis=0), jnp.concatenate(picks, axis=0)


def _candidate_blocks():
    blocks = [(0, 0, 16)]
    for a in range(1, 8):
        blocks.append((a, 0, 8))
    return blocks


def _route_kernel(hn_ref, wqt_ref, keys_ref, e_ref, g_ref, words_ref):
    hn = hn_ref[...].astype(jnp.bfloat16)
    qpt = lax.dot_general(wqt_ref[...], hn, (((1,), (1,)), ((), ())),
                          preferred_element_type=jnp.float32)
    qpt = qpt.astype(jnp.bfloat16)
    tm = hn.shape[0]
    half = PEER_DQ // 2
    words_ref[PEER_SEL // 2:, :] = jnp.zeros((LANES - PEER_SEL // 2, tm), jnp.int32)
    k0 = keys_ref[0]
    k1 = keys_ref[1]
    row_iota = lax.broadcasted_iota(jnp.int32, (N_KEYS, LANES), 0).astype(jnp.float32)

    blocks = _candidate_blocks()
    flat_parts = []
    for a, b0, nb in blocks:
        flat_parts.append(a * PEER_TOPK + b0 + lax.broadcasted_iota(jnp.int32, (nb, LANES), 0))
    tail_flat = (8 + lax.broadcasted_iota(jnp.int32, (8, LANES), 0)) * PEER_TOPK
    flat = jnp.concatenate(flat_parts + [tail_flat], axis=0).astype(jnp.float32)

    for h in range(PEER_HEADS):
        q1 = qpt[h * PEER_DQ:h * PEER_DQ + half, :]
        q2 = qpt[h * PEER_DQ + half:(h + 1) * PEER_DQ, :]
        s1 = jnp.dot(k0, q1, preferred_element_type=jnp.float32)
        s2 = jnp.dot(k1, q2, preferred_element_type=jnp.float32)
        for c in range(tm // LANES):
            sl = slice(c * LANES, (c + 1) * LANES)
            v1, i1 = _top_rows(s1[:, sl], row_iota, row_iota, PEER_TOPK)
            v2, i2 = _top_rows(s2[:, sl], row_iota, row_iota, PEER_TOPK)
            cs = []
            ce = []
            for a, b0, nb in blocks:
                cs.append(v1[a:a + 1, :] + v2[b0:b0 + nb, :])
                ce.append(i1[a:a + 1, :] * N_KEYS + i2[b0:b0 + nb, :])
            cs.append(v1[8:16, :] + v2[0:1, :])
            ce.append(i1[8:16, :] * N_KEYS + i2[0:1, :])
            cand_s = jnp.concatenate(cs, axis=0)
            cand_e = jnp.concatenate(ce, axis=0)
            top_s, top_e = _top_rows(cand_s, flat, cand_e, PEER_TOPK)
            ex = jnp.exp(top_s - top_s[0:1, :])
            gates = ex / jnp.sum(ex, axis=0, keepdims=True)
            off = top_e.astype(jnp.int32) * ROW_WORDS
            half_k = PEER_TOPK // 2
            words_ref[h * half_k:(h + 1) * half_k, sl] = (off[half_k:, :] << 16) | off[:half_k, :]
            g_ref[h, :, sl] = gates
    e_ref[...] = words_ref[...].T


def _route(hn, wqt, keys, tm):
    t, d = hn.shape
    return pl.pallas_call(
        _route_kernel,
        grid=(t // tm,),
        in_specs=[
            pl.BlockSpec((tm, d), lambda i: (i, 0)),
            pl.BlockSpec(wqt.shape, lambda i: (0, 0)),
            pl.BlockSpec(keys.shape, lambda i: (0, 0, 0)),
        ],
        out_specs=[
            pl.BlockSpec((tm, LANES), lambda i: (i, 0)),
            pl.BlockSpec((PEER_HEADS, PEER_TOPK, tm), lambda i: (0, 0, i)),
        ],
        out_shape=[
            jax.ShapeDtypeStruct((t, LANES), jnp.int32),
            jax.ShapeDtypeStruct((PEER_HEADS, PEER_TOPK, t), jnp.float32),
        ],
        scratch_shapes=[pltpu.VMEM((LANES, tm), jnp.int32)],
        compiler_params=_cparams(("parallel",)),
        name="route",
    )(hn, wqt, keys)


PEER_TOK = 128


def _expert_row(tbl_ref, off):
    words = tbl_ref[pl.ds(pl.multiple_of(off, ROW_WORDS), ROW_WORDS), :]
    return pltpu.bitcast(words, jnp.bfloat16).astype(jnp.float32)


def _expert_offsets(idx_row, head):
    lo, hi = [], []
    for j in range(PEER_TOPK // 2):
        word = idx_row[head * (PEER_TOPK // 2) + j]
        lo.append(word & 0xFFFF)
        hi.append(lax.shift_right_logical(word, 16))
    return lo + hi


def _gelu_tanh(x):
    c = np.float32(np.sqrt(2.0 / np.pi))
    return 0.5 * x * (1.0 + jnp.tanh(c * (x + 0.044715 * (x * x * x))))


def _fold_rows(rows, x, x_swapped, masks):
    lo4, mod4_lo2, even = masks
    a, e, c, g, b, f, d, h = rows

    def level1(p, q):
        t1 = jnp.where(lo4, p, q)
        t2 = pltpu.roll(jnp.where(lo4, q, p), 4, axis=0)
        return t1 * x + t2 * x_swapped

    def level2(u, v):
        uu = u + pltpu.roll(u, 6, axis=0)
        vv = v + pltpu.roll(v, 2, axis=0)
        return jnp.where(mod4_lo2, uu, vv)

    def level3(u, v):
        uu = u + pltpu.roll(u, 7, axis=0)
        vv = v + pltpu.roll(v, 1, axis=0)
        return jnp.where(even, uu, vv)

    return level3(level2(level1(a, b), level1(c, d)), level2(level1(e, f), level1(g, h)))


U_TOKENS_PER_STEP = 8


def _peer_u_kernel(e_ref, x_ref, gate_ref, tbl_ref, w_ref, s_ref, part_a, part_b):
    lane = lax.broadcasted_iota(jnp.int32, (PEER_SEL, PEER_TOK), 1)
    sub = lax.broadcasted_iota(jnp.int32, (SUBLANES, LANES), 0)
    masks = (sub < 4, (sub % 4) < 2, (sub % 2) == 0)

    def fold(t, part_ref):
        e_row = e_ref.at[t]
        x = x_ref[t]
        x_swapped = pltpu.roll(x, 4, axis=0)
        for head in range(PEER_HEADS):
            offs = _expert_offsets(e_row, head)
            for half in range(PEER_TOPK // SUBLANES):
                grp = head * (PEER_TOPK // SUBLANES) + half
                rows = [_expert_row(tbl_ref, offs[half * SUBLANES + j]) for j in range(SUBLANES)]
                part_ref[grp * SUBLANES:(grp + 1) * SUBLANES, :] = _fold_rows(rows, x, x_swapped, masks)

    def finish(t, part_ref):
        s = jnp.sum(part_ref[...], axis=1, keepdims=True)
        s_ref[...] = jnp.where(lane == t, s, s_ref[...])

    s_ref[...] = jnp.zeros(s_ref.shape, jnp.float32)
    part_b[...] = jnp.zeros(part_b.shape, jnp.float32)

    def token_block(i, carry):
        t0 = U_TOKENS_PER_STEP * i
        for j in range(U_TOKENS_PER_STEP):
            cur, prev = (part_a, part_b) if j % 2 == 0 else (part_b, part_a)
            fold(t0 + j, cur)
            finish(t0 + j - 1, prev)
        return carry

    lax.fori_loop(0, PEER_TOK // U_TOKENS_PER_STEP, token_block, 0)
    finish(PEER_TOK - 1, part_b)
    gates = gate_ref[...].reshape(PEER_SEL, PEER_TOK)
    w_ref[...] = gates * _gelu_tanh(s_ref[...])


def _peer_u(e_sm, x3, gates, tbl):
    t = x3.shape[0]
    return pl.pallas_call(
        _peer_u_kernel,
        grid=(t // PEER_TOK,),
        in_specs=[
            pl.BlockSpec((PEER_TOK, LANES), lambda i: (i, 0), memory_space=pltpu.SMEM),
            pl.BlockSpec((PEER_TOK, SUBLANES, LANES), lambda i: (i, 0, 0)),
            pl.BlockSpec((PEER_HEADS, PEER_TOPK, PEER_TOK), lambda i: (0, 0, i)),
            pl.BlockSpec(tbl.shape, lambda i: (0, 0), pipeline_mode=pl.Buffered(1)),
        ],
        out_specs=pl.BlockSpec((PEER_SEL, PEER_TOK), lambda i: (0, i)),
        out_shape=jax.ShapeDtypeStruct((PEER_SEL, t), jnp.float32),
        scratch_shapes=[
            pltpu.VMEM((PEER_SEL, PEER_TOK), jnp.float32),
            pltpu.VMEM((PEER_SEL, LANES), jnp.float32),
            pltpu.VMEM((PEER_SEL, LANES), jnp.float32),
        ],
        compiler_params=_cparams(("arbitrary",)),
        name="peer_u",
    )(e_sm, x3, gates, tbl)


V_TOKENS_PER_STEP = SUBLANES


def _peer_v_kernel(e_ref, wt_ref, x1_ref, fg_ref, tbl_ref, y_ref):
    lane = lax.broadcasted_iota(jnp.int32, (PEER_SEL, PEER_TOK), 1)

    def spread(t):
        col = jnp.sum(jnp.where(lane == t, wt_ref[...], 0.0), axis=1, keepdims=True)
        return jnp.broadcast_to(col, (PEER_SEL, LANES))

    def weighted_sum(t, wb, acc):
        e_row = e_ref.at[t]
        for head in range(PEER_HEADS):
            offs = _expert_offsets(e_row, head)
            for half in range(PEER_TOPK // SUBLANES):
                terms = []
                for j in range(SUBLANES):
                    slot = half * SUBLANES + j
                    k = head * PEER_TOPK + slot
                    terms.append(wb[k:k + 1, :] * _expert_row(tbl_ref, offs[slot]))
                while len(terms) > 1:
                    terms = [terms[i] + terms[i + 1] for i in range(0, len(terms), 2)]
                acc = acc + terms[0]
        return acc

    def token_group(i, wb):
        t0 = pl.multiple_of(i * V_TOKENS_PER_STEP, V_TOKENS_PER_STEP)
        x8 = x1_ref[pl.ds(t0, V_TOKENS_PER_STEP), :]
        tiles = []
        for j in range(V_TOKENS_PER_STEP):
            wb_next = spread(jnp.minimum(t0 + j + 1, PEER_TOK - 1))
            resid = jnp.concatenate([x8[j:j + 1, c * LANES:(c + 1) * LANES] for c in TILE_CHUNK], axis=0)
            tiles.append(weighted_sum(t0 + j, wb, resid))
            wb = wb_next
        for r, c in enumerate(TILE_CHUNK):
            rows = jnp.concatenate([tile[r:r + 1, :] for tile in tiles], axis=0)
            y_ref[pl.ds(t0, V_TOKENS_PER_STEP), c * LANES:(c + 1) * LANES] = rows
        return wb

    lax.fori_loop(0, PEER_TOK // V_TOKENS_PER_STEP, token_group, spread(0))
    x2 = y_ref[...]
    ms = jnp.mean(x2 * x2, axis=-1, keepdims=True)
    y_ref[...] = x2 * lax.rsqrt(ms + EPS) * fg_ref[...]


def _peer_v(e_sm, wt, x13, fg, tbl):
    t = x13.shape[0]
    return pl.pallas_call(
        _peer_v_kernel,
        grid=(t // PEER_TOK,),
        in_specs=[
            pl.BlockSpec((PEER_TOK, LANES), lambda i: (i, 0), memory_space=pltpu.SMEM),
            pl.BlockSpec((PEER_SEL, PEER_TOK), lambda i: (0, i)),
            pl.BlockSpec((PEER_TOK, ROW_TILE * LANES), lambda i: (i, 0)),
            pl.BlockSpec((1, ROW_TILE * LANES), lambda i: (0, 0)),
            pl.BlockSpec(tbl.shape, lambda i: (0, 0), pipeline_mode=pl.Buffered(1)),
        ],
        out_specs=pl.BlockSpec((PEER_TOK, ROW_TILE * LANES), lambda i: (i, 0)),
        out_shape=jax.ShapeDtypeStruct((t, ROW_TILE * LANES), jnp.float32),
        compiler_params=_cparams(("arbitrary",)),
        name="peer_v",
    )(e_sm, wt, x13, fg, tbl)


def _pack_table(tbl):
    n, d = tbl.shape
    assert d == ROW_TILE * LANES
    tb = tbl.astype(jnp.bfloat16)
    pairs = jnp.stack([tb[:, :d // 2], tb[:, d // 2:]], axis=-1)
    return lax.bitcast_convert_type(pairs, jnp.int32).reshape(n * ROW_WORDS, LANES)


def _rope_tables(seq):
    pos = jnp.arange(seq)
    row = (pos // GRID_W).astype(jnp.float32)
    col = (pos % GRID_W).astype(jnp.float32)
    n_pairs = ROPE_AXIS_DIM // 2
    inv_freq = ROPE_THETA ** (-jnp.arange(n_pairs, dtype=jnp.float32) / n_pairs)
    ang_r = row[:, None] * inv_freq[None, :]
    ang_c = col[:, None] * inv_freq[None, :]
    cos = jnp.concatenate([jnp.cos(ang_r), jnp.cos(ang_r), jnp.cos(ang_c), jnp.cos(ang_c)], axis=1)
    sin = jnp.concatenate([-jnp.sin(ang_r), jnp.sin(ang_r), -jnp.sin(ang_c), jnp.sin(ang_c)], axis=1)
    reps = LANES // HEAD_DIM
    return jnp.tile(cos, (1, reps)), jnp.tile(sin, (1, reps))


def kernel(x, norm1_g, w_in, q_norm_g, k_norm_g, conv_dw, conv_b, conv_ln_g, conv_ln_b, w_out,
           norm2_g, peer_wq, peer_keys, peer_u, peer_v, final_g):
    b, s, d = x.shape
    t = b * s
    assert d == SUBLANES * LANES and s % GRID_W == 0 and N_KV_HEADS == 2
    tm = min(512, s)
    tq = min(512, s)
    tk = min(1024, s)
    tr = min(256, s)
    assert s % tm == 0 and t % PEER_TOK == 0

    f32 = jnp.float32
    bf16 = jnp.bfloat16
    x2 = x.reshape(t, d)
    cos_t, sin_t = _rope_tables(s)
    head_id = jnp.arange(ATTN_WIDTH) // HEAD_DIM
    ones_bd = (head_id[:, None] == head_id[None, :]).astype(bf16)

    qt, k, vt, hg = _in_proj(
        x2, norm1_g.reshape(1, d), w_in.astype(bf16),
        jnp.tile(q_norm_g, N_HEADS).reshape(1, ATTN_WIDTH),
        jnp.tile(k_norm_g, N_KV_HEADS).reshape(1, KV_WIDTH),
        cos_t, sin_t, ones_bd, b, s, tm)

    attn = _attention(qt, k, vt, b, s, tq, tk)

    cw = jnp.concatenate([conv_dw.reshape(CONV_W, CONV_CH), jnp.zeros((1, CONV_CH), f32)], axis=0)
    x1, hn, hnt = _mix_out(attn, hg, cw, conv_b.reshape(1, CONV_CH), conv_ln_g.reshape(1, CONV_CH),
                      conv_ln_b.reshape(1, CONV_CH), w_out.astype(bf16), x2,
                      norm2_g.reshape(1, d), s, tm)

    e_sm, gates = _route(hn, peer_wq.T.astype(bf16), peer_keys.astype(bf16), tr)

    w = _peer_u(e_sm, hnt, gates, _pack_table(peer_u))
    y = _peer_v(e_sm, w, x1, final_g.reshape(1, d), _pack_table(peer_v))
    return y.reshape(b, s, d)
```

```python
import functools

import jax
import jax.numpy as jnp
import numpy as np
from jax import lax
from jax.experimental import pallas as pl
from jax.experimental.pallas import tpu as pltpu

HEAD_DIM = 64
N_HEADS = 8
N_KV_HEADS = 2
GROUP = N_HEADS // N_KV_HEADS
ATTN_WIDTH = N_HEADS * HEAD_DIM
KV_WIDTH = N_KV_HEADS * HEAD_DIM
CONV_CH = 512
CONV_W = 31
CONV_PAD = (CONV_W - 1) // 2
GRID_W = 64
ROPE_THETA = 10000.0
ROPE_AXIS_DIM = HEAD_DIM // 2
N_KEYS = 128
PEER_HEADS = 8
PEER_DQ = 256
PEER_TOPK = 16
PEER_SEL = PEER_HEADS * PEER_TOPK
EPS = 1e-6

LANES = 128
SUBLANES = 8
HALO = 16
BF16_ROWS = 16
V_EXT = HEAD_DIM + BF16_ROWS
LOG2_E = 1.4426950408889634
VMEM_LIMIT = 56 * 1024 * 1024

ROW_TILE = 8
ROW_WORDS = ROW_TILE // 2
TILE_CHUNK = tuple((r % 2) * ROW_WORDS + r // 2 for r in range(ROW_TILE))

_NEG_INF = float("-inf")


def _cparams(sem, vmem=VMEM_LIMIT):
    return pltpu.CompilerParams(dimension_semantics=sem, vmem_limit_bytes=vmem)


def _group_mean_sq(v, ones_blockdiag):
    sq = v * v
    hi = sq.astype(jnp.bfloat16)
    lo = (sq - hi.astype(jnp.float32)).astype(jnp.bfloat16)
    tot = (jnp.dot(hi, ones_blockdiag, preferred_element_type=jnp.float32)
           + jnp.dot(lo, ones_blockdiag, preferred_element_type=jnp.float32))
    return tot * (1.0 / HEAD_DIM)


def _rope(v, cos, sin_signed, lane_in_pair_lo):
    width = v.shape[-1]
    up = pltpu.roll(v, width - ROPE_AXIS_DIM // 2, axis=1)
    down = pltpu.roll(v, ROPE_AXIS_DIM // 2, axis=1)
    partner = jnp.where(lane_in_pair_lo, up, down)
    return v * cos + partner * sin_signed


def _in_proj_kernel(x_ref, g1_ref, w_ref, qg_ref, kg_ref, cos_ref, sin_ref, ones_ref,
                    qt_ref, k_ref, vt_ref, hg_ref):
    x = x_ref[...]
    ms = jnp.mean(x * x, axis=-1, keepdims=True)
    h = (x * lax.rsqrt(ms + EPS) * g1_ref[...]).astype(jnp.bfloat16)
    p = jnp.dot(h, w_ref[...], preferred_element_type=jnp.float32)
    o1 = ATTN_WIDTH
    o2 = o1 + KV_WIDTH
    o3 = o2 + KV_WIDTH
    o4 = o3 + CONV_CH
    q = p[:, :o1]
    k = p[:, o1:o2]
    v = p[:, o2:o3]
    a = p[:, o3:o4]
    gate = p[:, o4:]

    ones_bd = ones_ref[...]
    qn = q * lax.rsqrt(_group_mean_sq(q, ones_bd) + EPS) * qg_ref[...]
    kn = k * lax.rsqrt(_group_mean_sq(k, ones_bd[:KV_WIDTH, :KV_WIDTH]) + EPS) * kg_ref[...]

    cos2 = cos_ref[...]
    sin2 = sin_ref[...]
    cos_q = jnp.concatenate([cos2] * (ATTN_WIDTH // LANES), axis=1)
    sin_q = jnp.concatenate([sin2] * (ATTN_WIDTH // LANES), axis=1)
    lane_q = lax.broadcasted_iota(jnp.int32, q.shape, 1)
    lane_k = lax.broadcasted_iota(jnp.int32, k.shape, 1)
    half = ROPE_AXIS_DIM // 2
    qr = _rope(qn, cos_q, sin_q, (lane_q % ROPE_AXIS_DIM) < half) * (HEAD_DIM ** -0.5 * LOG2_E)
    kr = _rope(kn, cos2, sin2, (lane_k % ROPE_AXIS_DIM) < half)

    qt = qr.T.astype(qt_ref.dtype)
    zeros = jnp.zeros((HEAD_DIM, qt.shape[1]), qt_ref.dtype)
    for hh in range(N_HEADS):
        g = hh // GROUP
        piece = qt[hh * HEAD_DIM:(hh + 1) * HEAD_DIM, :]
        qt_ref[0, hh, g * HEAD_DIM:(g + 1) * HEAD_DIM, :] = piece
        qt_ref[0, hh, (1 - g) * HEAD_DIM:(2 - g) * HEAD_DIM, :] = zeros
    k_ref[...] = kr.astype(k_ref.dtype)
    vt = v.T.astype(vt_ref.dtype)
    pad_row = lax.broadcasted_iota(jnp.int32, (V_EXT - HEAD_DIM, vt.shape[1]), 0)
    ones_pad = jnp.where(pad_row == 0, 1.0, 0.0).astype(vt_ref.dtype)
    for g in range(N_KV_HEADS):
        vt_ref[0, g, 0:HEAD_DIM, :] = vt[g * HEAD_DIM:(g + 1) * HEAD_DIM, :]
        vt_ref[0, g, HEAD_DIM:V_EXT, :] = ones_pad
    hg_ref[...] = a * (1.0 / (1.0 + jnp.exp(-gate)))


def _in_proj(x2, g1, w_in, qg, kg, cos_t, sin_t, ones_bd, batch, seq, tm):
    t, d = x2.shape
    n_s = seq // tm
    in_width = w_in.shape[1]
    return pl.pallas_call(
        _in_proj_kernel,
        grid=(t // tm,),
        in_specs=[
            pl.BlockSpec((tm, d), lambda i: (i, 0)),
            pl.BlockSpec((1, d), lambda i: (0, 0)),
            pl.BlockSpec((d, in_width), lambda i: (0, 0)),
            pl.BlockSpec((1, ATTN_WIDTH), lambda i: (0, 0)),
            pl.BlockSpec((1, KV_WIDTH), lambda i: (0, 0)),
            pl.BlockSpec((tm, LANES), lambda i: (i % n_s, 0)),
            pl.BlockSpec((tm, LANES), lambda i: (i % n_s, 0)),
            pl.BlockSpec((ATTN_WIDTH, ATTN_WIDTH), lambda i: (0, 0)),
        ],
        out_specs=[
            pl.BlockSpec((1, N_HEADS, KV_WIDTH, tm), lambda i: (i // n_s, 0, 0, i % n_s)),
            pl.BlockSpec((tm, KV_WIDTH), lambda i: (i, 0)),
            pl.BlockSpec((1, N_KV_HEADS, V_EXT, tm), lambda i: (i // n_s, 0, 0, i % n_s)),
            pl.BlockSpec((tm, CONV_CH), lambda i: (i, 0)),
        ],
        out_shape=[
            jax.ShapeDtypeStruct((batch, N_HEADS, KV_WIDTH, seq), jnp.bfloat16),
            jax.ShapeDtypeStruct((t, KV_WIDTH), jnp.bfloat16),
            jax.ShapeDtypeStruct((batch, N_KV_HEADS, V_EXT, seq), jnp.bfloat16),
            jax.ShapeDtypeStruct((t, CONV_CH), jnp.float32),
        ],
        compiler_params=_cparams(("parallel",)),
        name="in_proj",
    )(x2, g1, w_in, qg, kg, cos_t, sin_t, ones_bd)


SCORES_AHEAD = 2


def _attn_kernel(qt_ref, k_ref, vt_ref, o_ref, m_ref, acc_ref):
    ki = pl.program_id(2)

    @pl.when(ki == 0)
    def _():
        m_ref[...] = jnp.full(m_ref.shape, _NEG_INF, jnp.float32)
        acc_ref[...] = jnp.zeros(acc_ref.shape, jnp.float32)

    k = k_ref[...]

    def scores(h):
        return jnp.dot(k, qt_ref[0, h], preferred_element_type=jnp.float32)

    ahead = [scores(h) for h in range(SCORES_AHEAD)]
    for h in range(N_HEADS):
        g = h // GROUP
        s = ahead.pop(0)
        if h + SCORES_AHEAD < N_HEADS:
            ahead.append(scores(h + SCORES_AHEAD))
        m_prev = m_ref[h:h + 1, :]
        m_new = jnp.maximum(m_prev, jnp.max(s, axis=0, keepdims=True))
        alpha = jnp.exp2(m_prev - m_new)
        p = jnp.exp2(s - m_new).astype(jnp.bfloat16)
        pv = jnp.dot(vt_ref[0, g], p, preferred_element_type=jnp.float32)
        acc_ref[h] = alpha * acc_ref[h] + pv
        m_ref[h:h + 1, :] = m_new

    @pl.when(ki == pl.num_programs(2) - 1)
    def _():
        outs = []
        for h in range(N_HEADS):
            acc = acc_ref[h]
            outs.append(acc[:HEAD_DIM, :] / acc[HEAD_DIM:HEAD_DIM + 1, :])
        o_ref[...] = jnp.concatenate(outs, axis=0).T.astype(o_ref.dtype)


def _attention(qt, k, vt, batch, seq, tq, tk):
    t = k.shape[0]
    nq = seq // tq
    nk = seq // tk
    return pl.pallas_call(
        _attn_kernel,
        grid=(batch, nq, nk),
        in_specs=[
            pl.BlockSpec((1, N_HEADS, KV_WIDTH, tq), lambda b, qi, ki: (b, 0, 0, qi)),
            pl.BlockSpec((tk, KV_WIDTH), lambda b, qi, ki: (b * nk + ki, 0)),
            pl.BlockSpec((1, N_KV_HEADS, V_EXT, tk), lambda b, qi, ki: (b, 0, 0, ki)),
        ],
        out_specs=pl.BlockSpec((tq, ATTN_WIDTH), lambda b, qi, ki: (b * nq + qi, 0)),
        out_shape=jax.ShapeDtypeStruct((t, ATTN_WIDTH), jnp.bfloat16),
        scratch_shapes=[
            pltpu.VMEM((N_HEADS, tq), jnp.float32),
            pltpu.VMEM((N_HEADS, V_EXT, tq), jnp.float32),
        ],
        compiler_params=_cparams(("parallel", "parallel", "arbitrary")),
        name="attention",
    )(qt, k, vt)


CONV_ROWS = 32


def _mix_out_kernel(attn_ref, hg_ref, prev_ref, next_ref, cw_ref, cb_ref, lg_ref, lb_ref,
                    wo_ref, x_ref, g2_ref, x1_ref, hn_ref, hnt_ref, ext_ref, conv_ref, *, n_s):
    tm = hg_ref.shape[0]
    si = pl.program_id(0) % n_s
    keep_prev = (si > 0).astype(jnp.float32)
    keep_next = (si < n_s - 1).astype(jnp.float32)
    ext_ref[0:HALO, :] = prev_ref[...] * keep_prev
    ext_ref[HALO:HALO + tm, :] = hg_ref[...]
    ext_ref[HALO + tm:HALO + tm + HALO, :] = next_ref[...] * keep_next

    cw = cw_ref[...]
    cb = cb_ref[...]
    lg = lg_ref[...]
    lb = lb_ref[...]

    def conv_step(c, carry):
        start = pl.multiple_of(c * CONV_ROWS, CONV_ROWS)
        rows = CONV_ROWS + 2 * HALO
        parts = []
        for lt in range(CONV_CH // LANES):
            cols = slice(lt * LANES, (lt + 1) * LANES)
            win = ext_ref[pl.ds(start, rows), cols]
            acc_l = jnp.zeros((CONV_ROWS, LANES), jnp.float32)
            for shift in range(SUBLANES):
                shifted = win if shift == 0 else pltpu.roll(win, rows - shift, axis=0)
                for j in range(CONV_W):
                    off = HALO - CONV_PAD + j
                    if off % SUBLANES == shift:
                        base = off - shift
                        acc_l = acc_l + shifted[base:base + CONV_ROWS, :] * cw[j:j + 1, cols]
            parts.append(acc_l)
        acc = jnp.concatenate(parts, axis=1)
        acc = acc + cb
        mu = jnp.mean(acc, axis=-1, keepdims=True)
        cen = acc - mu
        var = jnp.mean(cen * cen, axis=-1, keepdims=True)
        y = cen * lax.rsqrt(var + EPS) * lg + lb
        y = y * (1.0 / (1.0 + jnp.exp(-y)))
        conv_ref[pl.ds(start, CONV_ROWS), :] = y.astype(conv_ref.dtype)
        return carry

    lax.fori_loop(0, tm // CONV_ROWS, conv_step, 0)

    mixed = (jnp.dot(attn_ref[...], wo_ref[0:ATTN_WIDTH, :], preferred_element_type=jnp.float32)
             + jnp.dot(conv_ref[...], wo_ref[ATTN_WIDTH:, :], preferred_element_type=jnp.float32))
    x1 = x_ref[...] + mixed
    x1_ref[...] = x1
    ms = jnp.mean(x1 * x1, axis=-1, keepdims=True)
    hn = x1 * lax.rsqrt(ms + EPS) * g2_ref[...]
    hn_ref[...] = hn
    for r, c in enumerate(TILE_CHUNK):
        hnt_ref[:, r, :] = hn[:, c * LANES:(c + 1) * LANES]


def _mix_out(attn, hg, cw, cb, lg, lb, w_out, x2, g2, seq, tm):
    t, d = x2.shape
    n_s = seq // tm
    hb = tm // HALO
    last_halo = t // HALO - 1
    return pl.pallas_call(
        functools.partial(_mix_out_kernel, n_s=n_s),
        grid=(t // tm,),
        in_specs=[
            pl.BlockSpec((tm, ATTN_WIDTH), lambda i: (i, 0)),
            pl.BlockSpec((tm, CONV_CH), lambda i: (i, 0)),
            pl.BlockSpec((HALO, CONV_CH), lambda i: (jnp.maximum(i * hb - 1, 0), 0)),
            pl.BlockSpec((HALO, CONV_CH), lambda i: (jnp.minimum((i + 1) * hb, last_halo), 0)),
            pl.BlockSpec((CONV_W + 1, CONV_CH), lambda i: (0, 0)),
            pl.BlockSpec((1, CONV_CH), lambda i: (0, 0)),
            pl.BlockSpec((1, CONV_CH), lambda i: (0, 0)),
            pl.BlockSpec((1, CONV_CH), lambda i: (0, 0)),
            pl.BlockSpec((d, d), lambda i: (0, 0)),
            pl.BlockSpec((tm, d), lambda i: (i, 0)),
            pl.BlockSpec((1, d), lambda i: (0, 0)),
        ],
        out_specs=[
            pl.BlockSpec((tm, d), lambda i: (i, 0)),
            pl.BlockSpec((tm, d), lambda i: (i, 0)),
            pl.BlockSpec((tm, ROW_TILE, LANES), lambda i: (i, 0, 0)),
        ],
        out_shape=[
            jax.ShapeDtypeStruct((t, d), jnp.float32),
            jax.ShapeDtypeStruct((t, d), jnp.float32),
            jax.ShapeDtypeStruct((t, ROW_TILE, LANES), jnp.float32),
        ],
        scratch_shapes=[
            pltpu.VMEM((tm + 2 * HALO, CONV_CH), jnp.float32),
            pltpu.VMEM((tm, CONV_CH), jnp.bfloat16),
        ],
        compiler_params=_cparams(("parallel",)),
        name="mix_out",
    )(attn, hg, hg, hg, cw, cb, lg, lb, w_out, x2, g2)


def _top_rows(s, order, payload, count):
    big = jnp.float32(2 ** 30)
    vals = []
    picks = []
    for _ in range(count):
        m = jnp.max(s, axis=0, keepdims=True)
        first = jnp.min(jnp.where(s == m, order, big), axis=0, keepdims=True)
        chosen = order == first
        if payload is order:
            pick = first
        else:
            pick = jnp.max(jnp.where(chosen, payload, -1.0), axis=0, keepdims=True)
        vals.append(m)
        picks.append(pick)
        s = jnp.where(chosen, _NEG_INF, s)
    return jnp.concatenate(vals, axis=0), jnp.concatenate(picks, axis=0)


def _candidate_blocks():
    blocks = [(0, 0, 16)]
    for a in range(1, 8):
        blocks.append((a, 0, 8))
    return blocks


def _route_kernel(hn_ref, wqt_ref, keys_ref, e_ref, g_ref, words_ref):
    hn = hn_ref[...].astype(jnp.bfloat16)
    qpt = lax.dot_general(wqt_ref[...], hn, (((1,), (1,)), ((), ())),
                          preferred_element_type=jnp.float32)
    qpt = qpt.astype(jnp.bfloat16)
    tm = hn.shape[0]
    half = PEER_DQ // 2
    words_ref[PEER_SEL // 2:, :] = jnp.zeros((LANES - PEER_SEL // 2, tm), jnp.int32)
    k0 = keys_ref[0]
    k1 = keys_ref[1]
    row_iota = lax.broadcasted_iota(jnp.int32, (N_KEYS, LANES), 0).astype(jnp.float32)

    blocks = _candidate_blocks()
    flat_parts = []
    for a, b0, nb in blocks:
        flat_parts.append(a * PEER_TOPK + b0 + lax.broadcasted_iota(jnp.int32, (nb, LANES), 0))
    tail_flat = (8 + lax.broadcasted_iota(jnp.int32, (8, LANES), 0)) * PEER_TOPK
    flat = jnp.concatenate(flat_parts + [tail_flat], axis=0).astype(jnp.float32)

    for h in range(PEER_HEADS):
        q1 = qpt[h * PEER_DQ:h * PEER_DQ + half, :]
        q2 = qpt[h * PEER_DQ + half:(h + 1) * PEER_DQ, :]
        s1 = jnp.dot(k0, q1, preferred_element_type=jnp.float32)
        s2 = jnp.dot(k1, q2, preferred_element_type=jnp.float32)
        for c in range(tm // LANES):
            sl = slice(c * LANES, (c + 1) * LANES)
            v1, i1 = _top_rows(s1[:, sl], row_iota, row_iota, PEER_TOPK)
            v2, i2 = _top_rows(s2[:, sl], row_iota, row_iota, PEER_TOPK)
            cs = []
            ce = []
            for a, b0, nb in blocks:
                cs.append(v1[a:a + 1, :] + v2[b0:b0 + nb, :])
                ce.append(i1[a:a + 1, :] * N_KEYS + i2[b0:b0 + nb, :])
            cs.append(v1[8:16, :] + v2[0:1, :])
            ce.append(i1[8:16, :] * N_KEYS + i2[0:1, :])
            cand_s = jnp.concatenate(cs, axis=0)
            cand_e = jnp.concatenate(ce, axis=0)
            top_s, top_e = _top_rows(cand_s, flat, cand_e, PEER_TOPK)
            ex = jnp.exp(top_s - top_s[0:1, :])
            gates = ex / jnp.sum(ex, axis=0, keepdims=True)
            off = top_e.astype(jnp.int32) * ROW_WORDS
            half_k = PEER_TOPK // 2
            words_ref[h * half_k:(h + 1) * half_k, sl] = (off[half_k:, :] << 16) | off[:half_k, :]
            g_ref[h, :, sl] = gates
    e_ref[...] = words_ref[...].T


def _route(hn, wqt, keys, tm):
    t, d = hn.shape
    return pl.pallas_call(
        _route_kernel,
        grid=(t // tm,),
        in_specs=[
            pl.BlockSpec((tm, d), lambda i: (i, 0)),
            pl.BlockSpec(wqt.shape, lambda i: (0, 0)),
            pl.BlockSpec(keys.shape, lambda i: (0, 0, 0)),
        ],
        out_specs=[
            pl.BlockSpec((tm, LANES), lambda i: (i, 0)),
            pl.BlockSpec((PEER_HEADS, PEER_TOPK, tm), lambda i: (0, 0, i)),
        ],
        out_shape=[
            jax.ShapeDtypeStruct((t, LANES), jnp.int32),
            jax.ShapeDtypeStruct((PEER_HEADS, PEER_TOPK, t), jnp.float32),
        ],
        scratch_shapes=[pltpu.VMEM((LANES, tm), jnp.int32)],
        compiler_params=_cparams(("parallel",)),
        name="route",
    )(hn, wqt, keys)


PEER_TOK = 128


def _expert_row(tbl_ref, off):
    words = tbl_ref[pl.ds(pl.multiple_of(off, ROW_WORDS), ROW_WORDS), :]
    return pltpu.bitcast(words, jnp.bfloat16).astype(jnp.float32)


def _expert_offsets(idx_row, head):
    lo, hi = [], []
    for j in range(PEER_TOPK // 2):
        word = idx_row[head * (PEER_TOPK // 2) + j]
        lo.append(word & 0xFFFF)
        hi.append(lax.shift_right_logical(word, 16))
    return lo + hi


def _gelu_tanh(x):
    c = np.float32(np.sqrt(2.0 / np.pi))
    return 0.5 * x * (1.0 + jnp.tanh(c * (x + 0.044715 * (x * x * x))))


def _fold_rows(rows, x, x_swapped, masks):
    lo4, mod4_lo2, even = masks
    a, e, c, g, b, f, d, h = rows

    def level1(p, q):
        t1 = jnp.where(lo4, p, q)
        t2 = pltpu.roll(jnp.where(lo4, q, p), 4, axis=0)
        return t1 * x + t2 * x_swapped

    def level2(u, v):
        uu = u + pltpu.roll(u, 6, axis=0)
        vv = v + pltpu.roll(v, 2, axis=0)
        return jnp.where(mod4_lo2, uu, vv)

    def level3(u, v):
        uu = u + pltpu.roll(u, 7, axis=0)
        vv = v + pltpu.roll(v, 1, axis=0)
        return jnp.where(even, uu, vv)

    return level3(level2(level1(a, b), level1(c, d)), level2(level1(e, f), level1(g, h)))


U_TOKENS_PER_STEP = 16


def _peer_u_kernel(e_ref, x_ref, gate_ref, tbl_ref, w_ref, s_ref, part_a, part_b):
    lane = lax.broadcasted_iota(jnp.int32, (PEER_SEL, PEER_TOK), 1)
    sub = lax.broadcasted_iota(jnp.int32, (SUBLANES, LANES), 0)
    masks = (sub < 4, (sub % 4) < 2, (sub % 2) == 0)

    def fold(t, part_ref):
        e_row = e_ref.at[t]
        x = x_ref[t]
        x_swapped = pltpu.roll(x, 4, axis=0)
        for head in range(PEER_HEADS):
            offs = _expert_offsets(e_row, head)
            for half in range(PEER_TOPK // SUBLANES):
                grp = head * (PEER_TOPK // SUBLANES) + half
                rows = [_expert_row(tbl_ref, offs[half * SUBLANES + j]) for j in range(SUBLANES)]
                part_ref[grp * SUBLANES:(grp + 1) * SUBLANES, :] = _fold_rows(rows, x, x_swapped, masks)

    def finish(t, part_ref):
        s = jnp.sum(part_ref[...], axis=1, keepdims=True)
        s_ref[...] = jnp.where(lane == t, s, s_ref[...])

    s_ref[...] = jnp.zeros(s_ref.shape, jnp.float32)
    part_b[...] = jnp.zeros(part_b.shape, jnp.float32)

    def token_block(i, carry):
        t0 = U_TOKENS_PER_STEP * i
        for j in range(U_TOKENS_PER_STEP):
            cur, prev = (part_a, part_b) if j % 2 == 0 else (part_b, part_a)
            fold(t0 + j, cur)
            finish(t0 + j - 1, prev)
        return carry

    lax.fori_loop(0, PEER_TOK // U_TOKENS_PER_STEP, token_block, 0)
    finish(PEER_TOK - 1, part_b)
    gates = gate_ref[...].reshape(PEER_SEL, PEER_TOK)
    w_ref[...] = gates * _gelu_tanh(s_ref[...])


def _peer_u(e_sm, x3, gates, tbl):
    t = x3.shape[0]
    return pl.pallas_call(
        _peer_u_kernel,
        grid=(t // PEER_TOK,),
        in_specs=[
            pl.BlockSpec((PEER_TOK, LANES), lambda i: (i, 0), memory_space=pltpu.SMEM),
            pl.BlockSpec((PEER_TOK, SUBLANES, LANES), lambda i: (i, 0, 0)),
            pl.BlockSpec((PEER_HEADS, PEER_TOPK, PEER_TOK), lambda i: (0, 0, i)),
            pl.BlockSpec(tbl.shape, lambda i: (0, 0), pipeline_mode=pl.Buffered(1)),
        ],
        out_specs=pl.BlockSpec((PEER_SEL, PEER_TOK), lambda i: (0, i)),
        out_shape=jax.ShapeDtypeStruct((PEER_SEL, t), jnp.float32),
        scratch_shapes=[
            pltpu.VMEM((PEER_SEL, PEER_TOK), jnp.float32),
            pltpu.VMEM((PEER_SEL, LANES), jnp.float32),
            pltpu.VMEM((PEER_SEL, LANES), jnp.float32),
        ],
        compiler_params=_cparams(("arbitrary",)),
        name="peer_u",
    )(e_sm, x3, gates, tbl)


V_TOKENS_PER_STEP = SUBLANES


def _peer_v_kernel(e_ref, wt_ref, x1_ref, fg_ref, tbl_ref, y_ref):
    lane = lax.broadcasted_iota(jnp.int32, (PEER_SEL, PEER_TOK), 1)

    def spread(t):
        col = jnp.sum(jnp.where(lane == t, wt_ref[...], 0.0), axis=1, keepdims=True)
        return jnp.broadcast_to(col, (PEER_SEL, LANES))

    def weighted_sum(t, wb, acc):
        e_row = e_ref.at[t]
        for head in range(PEER_HEADS):
            offs = _expert_offsets(e_row, head)
            for half in range(PEER_TOPK // SUBLANES):
                terms = []
                for j in range(SUBLANES):
                    slot = half * SUBLANES + j
                    k = head * PEER_TOPK + slot
                    terms.append(wb[k:k + 1, :] * _expert_row(tbl_ref, offs[slot]))
                while len(terms) > 1:
                    terms = [terms[i] + terms[i + 1] for i in range(0, len(terms), 2)]
                acc = acc + terms[0]
        return acc

    def token_group(i, wb):
        t0 = pl.multiple_of(i * V_TOKENS_PER_STEP, V_TOKENS_PER_STEP)
        x8 = x1_ref[pl.ds(t0, V_TOKENS_PER_STEP), :]
        tiles = []
        for j in range(V_TOKENS_PER_STEP):
            wb_next = spread(jnp.minimum(t0 + j + 1, PEER_TOK - 1))
            resid = jnp.concatenate([x8[j:j + 1, c * LANES:(c + 1) * LANES] for c in TILE_CHUNK], axis=0)
            tiles.append(weighted_sum(t0 + j, wb, resid))
            wb = wb_next
        for r, c in enumerate(TILE_CHUNK):
            rows = jnp.concatenate([tile[r:r + 1, :] for tile in tiles], axis=0)
            y_ref[pl.ds(t0, V_TOKENS_PER_STEP), c * LANES:(c + 1) * LANES] = rows
        return wb

    lax.fori_loop(0, PEER_TOK // V_TOKENS_PER_STEP, token_group, spread(0))
    x2 = y_ref[...]
    ms = jnp.mean(x2 * x2, axis=-1, keepdims=True)
    y_ref[...] = x2 * lax.rsqrt(ms + EPS) * fg_ref[...]


def _peer_v(e_sm, wt, x13, fg, tbl):
    t = x13.shape[0]
    return pl.pallas_call(
        _peer_v_kernel,
        grid=(t // PEER_TOK,),
        in_specs=[
            pl.BlockSpec((PEER_TOK, LANES), lambda i: (i, 0), memory_space=pltpu.SMEM),
            pl.BlockSpec((PEER_SEL, PEER_TOK), lambda i: (0, i)),
            pl.BlockSpec((PEER_TOK, ROW_TILE * LANES), lambda i: (i, 0)),
            pl.BlockSpec((1, ROW_TILE * LANES), lambda i: (0, 0)),
            pl.BlockSpec(tbl.shape, lambda i: (0, 0), pipeline_mode=pl.Buffered(1)),
        ],
        out_specs=pl.BlockSpec((PEER_TOK, ROW_TILE * LANES), lambda i: (i, 0)),
        out_shape=jax.ShapeDtypeStruct((t, ROW_TILE * LANES), jnp.float32),
        compiler_params=_cparams(("arbitrary",)),
        name="peer_v",
    )(e_sm, wt, x13, fg, tbl)


def _pack_table(tbl):
    n, d = tbl.shape
    assert d == ROW_TILE * LANES
    tb = tbl.astype(jnp.bfloat16)
    pairs = jnp.stack([tb[:, :d // 2], tb[:, d // 2:]], axis=-1)
    return lax.bitcast_convert_type(pairs, jnp.int32).reshape(n * ROW_WORDS, LANES)


def _rope_tables(seq):
    pos = jnp.arange(seq)
    row = (pos // GRID_W).astype(jnp.float32)
    col = (pos % GRID_W).astype(jnp.float32)
    n_pairs = ROPE_AXIS_DIM // 2
    inv_freq = ROPE_THETA ** (-jnp.arange(n_pairs, dtype=jnp.float32) / n_pairs)
    ang_r = row[:, None] * inv_freq[None, :]
    ang_c = col[:, None] * inv_freq[None, :]
    cos = jnp.concatenate([jnp.cos(ang_r), jnp.cos(ang_r), jnp.cos(ang_c), jnp.cos(ang_c)], axis=1)
    sin = jnp.concatenate([-jnp.sin(ang_r), jnp.sin(ang_r), -jnp.sin(ang_c), jnp.sin(ang_c)], axis=1)
    reps = LANES // HEAD_DIM
    return jnp.tile(cos, (1, reps)), jnp.tile(sin, (1, reps))


def kernel(x, norm1_g, w_in, q_norm_g, k_norm_g, conv_dw, conv_b, conv_ln_g, conv_ln_b, w_out,
           norm2_g, peer_wq, peer_keys, peer_u, peer_v, final_g):
    b, s, d = x.shape
    t = b * s
    assert d == SUBLANES * LANES and s % GRID_W == 0 and N_KV_HEADS == 2
    tm = min(512, s)
    tq = min(512, s)
    tk = min(1024, s)
    tr = min(256, s)
    assert s % tm == 0 and t % PEER_TOK == 0

    f32 = jnp.float32
    bf16 = jnp.bfloat16
    x2 = x.reshape(t, d)
    cos_t, sin_t = _rope_tables(s)
    head_id = jnp.arange(ATTN_WIDTH) // HEAD_DIM
    ones_bd = (head_id[:, None] == head_id[None, :]).astype(bf16)

    qt, k, vt, hg = _in_proj(
        x2, norm1_g.reshape(1, d), w_in.astype(bf16),
        jnp.tile(q_norm_g, N_HEADS).reshape(1, ATTN_WIDTH),
        jnp.tile(k_norm_g, N_KV_HEADS).reshape(1, KV_WIDTH),
        cos_t, sin_t, ones_bd, b, s, tm)

    attn = _attention(qt, k, vt, b, s, tq, tk)

    cw = jnp.concatenate([conv_dw.reshape(CONV_W, CONV_CH), jnp.zeros((1, CONV_CH), f32)], axis=0)
    x1, hn, hnt = _mix_out(attn, hg, cw, conv_b.reshape(1, CONV_CH), conv_ln_g.reshape(1, CONV_CH),
                      conv_ln_b.reshape(1, CONV_CH), w_out.astype(bf16), x2,
                      norm2_g.reshape(1, d), s, tm)

    e_sm, gates = _route(hn, peer_wq.T.astype(bf16), peer_keys.astype(bf16), tr)

    w = _peer_u(e_sm, hnt, gates, _pack_table(peer_u))
    y = _peer_v(e_sm, w, x1, final_g.reshape(1, d), _pack_table(peer_v))
    return y.reshape(b, s, d)
```

```python
import functools

import jax
import jax.numpy as jnp
import numpy as np
from jax import lax
from jax.experimental import pallas as pl
from jax.experimental.pallas import tpu as pltpu

HEAD_DIM = 64
N_HEADS = 8
N_KV_HEADS = 2
GROUP = N_HEADS // N_KV_HEADS
ATTN_WIDTH = N_HEADS * HEAD_DIM
KV_WIDTH = N_KV_HEADS * HEAD_DIM
CONV_CH = 512
CONV_W = 31
CONV_PAD = (CONV_W - 1) // 2
GRID_W = 64
ROPE_THETA = 10000.0
ROPE_AXIS_DIM = HEAD_DIM // 2
N_KEYS = 128
PEER_HEADS = 8
PEER_DQ = 256
PEER_TOPK = 16
PEER_SEL = PEER_HEADS * PEER_TOPK
EPS = 1e-6

LANES = 128
SUBLANES = 8
HALO = 16
BF16_ROWS = 16
V_EXT = HEAD_DIM + BF16_ROWS
LOG2_E = 1.4426950408889634
VMEM_LIMIT = 56 * 1024 * 1024

ROW_TILE = 8
ROW_WORDS = ROW_TILE // 2
TILE_CHUNK = tuple((r % 2) * ROW_WORDS + r // 2 for r in range(ROW_TILE))

_NEG_INF = float("-inf")


def _cparams(sem, vmem=VMEM_LIMIT):
    return pltpu.CompilerParams(dimension_semantics=sem, vmem_limit_bytes=vmem)


def _group_mean_sq(v, ones_blockdiag):
    sq = v * v
    hi = sq.astype(jnp.bfloat16)
    lo = (sq - hi.astype(jnp.float32)).astype(jnp.bfloat16)
    tot = (jnp.dot(hi, ones_blockdiag, preferred_element_type=jnp.float32)
           + jnp.dot(lo, ones_blockdiag, preferred_element_type=jnp.float32))
    return tot * (1.0 / HEAD_DIM)


def _rope(v, cos, sin_signed, lane_in_pair_lo):
    width = v.shape[-1]
    up = pltpu.roll(v, width - ROPE_AXIS_DIM // 2, axis=1)
    down = pltpu.roll(v, ROPE_AXIS_DIM // 2, axis=1)
    partner = jnp.where(lane_in_pair_lo, up, down)
    return v * cos + partner * sin_signed


def _in_proj_kernel(x_ref, g1_ref, w_ref, qg_ref, kg_ref, cos_ref, sin_ref, ones_ref,
                    qt_ref, k_ref, vt_ref, hg_ref):
    x = x_ref[...]
    ms = jnp.mean(x * x, axis=-1, keepdims=True)
    h = (x * lax.rsqrt(ms + EPS) * g1_ref[...]).astype(jnp.bfloat16)
    p = jnp.dot(h, w_ref[...], preferred_element_type=jnp.float32)
    o1 = ATTN_WIDTH
    o2 = o1 + KV_WIDTH
    o3 = o2 + KV_WIDTH
    o4 = o3 + CONV_CH
    q = p[:, :o1]
    k = p[:, o1:o2]
    v = p[:, o2:o3]
    a = p[:, o3:o4]
    gate = p[:, o4:]

    ones_bd = ones_ref[...]
    qn = q * lax.rsqrt(_group_mean_sq(q, ones_bd) + EPS) * qg_ref[...]
    kn = k * lax.rsqrt(_group_mean_sq(k, ones_bd[:KV_WIDTH, :KV_WIDTH]) + EPS) * kg_ref[...]

    cos2 = cos_ref[...]
    sin2 = sin_ref[...]
    cos_q = jnp.concatenate([cos2] * (ATTN_WIDTH // LANES), axis=1)
    sin_q = jnp.concatenate([sin2] * (ATTN_WIDTH // LANES), axis=1)
    lane_q = lax.broadcasted_iota(jnp.int32, q.shape, 1)
    lane_k = lax.broadcasted_iota(jnp.int32, k.shape, 1)
    half = ROPE_AXIS_DIM // 2
    qr = _rope(qn, cos_q, sin_q, (lane_q % ROPE_AXIS_DIM) < half) * (HEAD_DIM ** -0.5 * LOG2_E)
    kr = _rope(kn, cos2, sin2, (lane_k % ROPE_AXIS_DIM) < half)

    qt = qr.T.astype(qt_ref.dtype)
    zeros = jnp.zeros((HEAD_DIM, qt.shape[1]), qt_ref.dtype)
    for hh in range(N_HEADS):
        g = hh // GROUP
        piece = qt[hh * HEAD_DIM:(hh + 1) * HEAD_DIM, :]
        qt_ref[0, hh, g * HEAD_DIM:(g + 1) * HEAD_DIM, :] = piece
        qt_ref[0, hh, (1 - g) * HEAD_DIM:(2 - g) * HEAD_DIM, :] = zeros
    k_ref[...] = kr.astype(k_ref.dtype)
    vt = v.T.astype(vt_ref.dtype)
    pad_row = lax.broadcasted_iota(jnp.int32, (V_EXT - HEAD_DIM, vt.shape[1]), 0)
    ones_pad = jnp.where(pad_row == 0, 1.0, 0.0).astype(vt_ref.dtype)
    for g in range(N_KV_HEADS):
        vt_ref[0, g, 0:HEAD_DIM, :] = vt[g * HEAD_DIM:(g + 1) * HEAD_DIM, :]
        vt_ref[0, g, HEAD_DIM:V_EXT, :] = ones_pad
    hg_ref[...] = a * (1.0 / (1.0 + jnp.exp(-gate)))


def _in_proj(x2, g1, w_in, qg, kg, cos_t, sin_t, ones_bd, batch, seq, tm):
    t, d = x2.shape
    n_s = seq // tm
    in_width = w_in.shape[1]
    return pl.pallas_call(
        _in_proj_kernel,
        grid=(t // tm,),
        in_specs=[
            pl.BlockSpec((tm, d), lambda i: (i, 0)),
            pl.BlockSpec((1, d), lambda i: (0, 0)),
            pl.BlockSpec((d, in_width), lambda i: (0, 0)),
            pl.BlockSpec((1, ATTN_WIDTH), lambda i: (0, 0)),
            pl.BlockSpec((1, KV_WIDTH), lambda i: (0, 0)),
            pl.BlockSpec((tm, LANES), lambda i: (i % n_s, 0)),
            pl.BlockSpec((tm, LANES), lambda i: (i % n_s, 0)),
            pl.BlockSpec((ATTN_WIDTH, ATTN_WIDTH), lambda i: (0, 0)),
        ],
        out_specs=[
            pl.BlockSpec((1, N_HEADS, KV_WIDTH, tm), lambda i: (i // n_s, 0, 0, i % n_s)),
            pl.BlockSpec((tm, KV_WIDTH), lambda i: (i, 0)),
            pl.BlockSpec((1, N_KV_HEADS, V_EXT, tm), lambda i: (i // n_s, 0, 0, i % n_s)),
            pl.BlockSpec((tm, CONV_CH), lambda i: (i, 0)),
        ],
        out_shape=[
            jax.ShapeDtypeStruct((batch, N_HEADS, KV_WIDTH, seq), jnp.bfloat16),
            jax.ShapeDtypeStruct((t, KV_WIDTH), jnp.bfloat16),
            jax.ShapeDtypeStruct((batch, N_KV_HEADS, V_EXT, seq), jnp.bfloat16),
            jax.ShapeDtypeStruct((t, CONV_CH), jnp.float32),
        ],
        compiler_params=_cparams(("parallel",)),
        name="in_proj",
    )(x2, g1, w_in, qg, kg, cos_t, sin_t, ones_bd)


SCORES_AHEAD = 2


def _attn_kernel(qt_ref, k_ref, vt_ref, o_ref, m_ref, acc_ref):
    ki = pl.program_id(2)

    @pl.when(ki == 0)
    def _():
        m_ref[...] = jnp.full(m_ref.shape, _NEG_INF, jnp.float32)
        acc_ref[...] = jnp.zeros(acc_ref.shape, jnp.float32)

    k = k_ref[...]

    def scores(h):
        return jnp.dot(k, qt_ref[0, h], preferred_element_type=jnp.float32)

    ahead = [scores(h) for h in range(SCORES_AHEAD)]
    for h in range(N_HEADS):
        g = h // GROUP
        s = ahead.pop(0)
        if h + SCORES_AHEAD < N_HEADS:
            ahead.append(scores(h + SCORES_AHEAD))
        m_prev = m_ref[h:h + 1, :]
        m_new = jnp.maximum(m_prev, jnp.max(s, axis=0, keepdims=True))
        alpha = jnp.exp2(m_prev - m_new)
        p = jnp.exp2(s - m_new).astype(jnp.bfloat16)
        pv = jnp.dot(vt_ref[0, g], p, preferred_element_type=jnp.float32)
        acc_ref[h] = alpha * acc_ref[h] + pv
        m_ref[h:h + 1, :] = m_new

    @pl.when(ki == pl.num_programs(2) - 1)
    def _():
        outs = []
        for h in range(N_HEADS):
            acc = acc_ref[h]
            outs.append(acc[:HEAD_DIM, :] / acc[HEAD_DIM:HEAD_DIM + 1, :])
        o_ref[...] = jnp.concatenate(outs, axis=0).T.astype(o_ref.dtype)


def _attention(qt, k, vt, batch, seq, tq, tk):
    t = k.shape[0]
    nq = seq // tq
    nk = seq // tk
    return pl.pallas_call(
        _attn_kernel,
        grid=(batch, nq, nk),
        in_specs=[
            pl.BlockSpec((1, N_HEADS, KV_WIDTH, tq), lambda b, qi, ki: (b, 0, 0, qi)),
            pl.BlockSpec((tk, KV_WIDTH), lambda b, qi, ki: (b * nk + ki, 0)),
            pl.BlockSpec((1, N_KV_HEADS, V_EXT, tk), lambda b, qi, ki: (b, 0, 0, ki)),
        ],
        out_specs=pl.BlockSpec((tq, ATTN_WIDTH), lambda b, qi, ki: (b * nq + qi, 0)),
        out_shape=jax.ShapeDtypeStruct((t, ATTN_WIDTH), jnp.bfloat16),
        scratch_shapes=[
            pltpu.VMEM((N_HEADS, tq), jnp.float32),
            pltpu.VMEM((N_HEADS, V_EXT, tq), jnp.float32),
        ],
        compiler_params=_cparams(("parallel", "parallel", "arbitrary")),
        name="attention",
    )(qt, k, vt)


CONV_ROWS = 32


def _mix_out_kernel(attn_ref, hg_ref, prev_ref, next_ref, cw_ref, cb_ref, lg_ref, lb_ref,
                    wo_ref, x_ref, g2_ref, x1_ref, hn_ref, hnt_ref, ext_ref, conv_ref, *, n_s):
    tm = hg_ref.shape[0]
    si = pl.program_id(0) % n_s
    keep_prev = (si > 0).astype(jnp.float32)
    keep_next = (si < n_s - 1).astype(jnp.float32)
    ext_ref[0:HALO, :] = prev_ref[...] * keep_prev
    ext_ref[HALO:HALO + tm, :] = hg_ref[...]
    ext_ref[HALO + tm:HALO + tm + HALO, :] = next_ref[...] * keep_next

    cw = cw_ref[...]
    cb = cb_ref[...]
    lg = lg_ref[...]
    lb = lb_ref[...]

    def conv_step(c, carry):
        start = pl.multiple_of(c * CONV_ROWS, CONV_ROWS)
        rows = CONV_ROWS + 2 * HALO
        parts = []
        for lt in range(CONV_CH // LANES):
            cols = slice(lt * LANES, (lt + 1) * LANES)
            win = ext_ref[pl.ds(start, rows), cols]
            acc_l = jnp.zeros((CONV_ROWS, LANES), jnp.float32)
            for shift in range(SUBLANES):
                shifted = win if shift == 0 else pltpu.roll(win, rows - shift, axis=0)
                for j in range(CONV_W):
                    off = HALO - CONV_PAD + j
                    if off % SUBLANES == shift:
                        base = off - shift
                        acc_l = acc_l + shifted[base:base + CONV_ROWS, :] * cw[j:j + 1, cols]
            parts.append(acc_l)
        acc = jnp.concatenate(parts, axis=1)
        acc = acc + cb
        mu = jnp.mean(acc, axis=-1, keepdims=True)
        cen = acc - mu
        var = jnp.mean(cen * cen, axis=-1, keepdims=True)
        y = cen * lax.rsqrt(var + EPS) * lg + lb
        y = y * (1.0 / (1.0 + jnp.exp(-y)))
        conv_ref[pl.ds(start, CONV_ROWS), :] = y.astype(conv_ref.dtype)
        return carry

    lax.fori_loop(0, tm // CONV_ROWS, conv_step, 0)

    mixed = (jnp.dot(attn_ref[...], wo_ref[0:ATTN_WIDTH, :], preferred_element_type=jnp.float32)
             + jnp.dot(conv_ref[...], wo_ref[ATTN_WIDTH:, :], preferred_element_type=jnp.float32))
    x1 = x_ref[...] + mixed
    x1_ref[...] = x1
    ms = jnp.mean(x1 * x1, axis=-1, keepdims=True)
    hn = x1 * lax.rsqrt(ms + EPS) * g2_ref[...]
    hn_ref[...] = hn
    for r, c in enumerate(TILE_CHUNK):
        hnt_ref[:, r, :] = hn[:, c * LANES:(c + 1) * LANES]


def _mix_out(attn, hg, cw, cb, lg, lb, w_out, x2, g2, seq, tm):
    t, d = x2.shape
    n_s = seq // tm
    hb = tm // HALO
    last_halo = t // HALO - 1
    return pl.pallas_call(
        functools.partial(_mix_out_kernel, n_s=n_s),
        grid=(t // tm,),
        in_specs=[
            pl.BlockSpec((tm, ATTN_WIDTH), lambda i: (i, 0)),
            pl.BlockSpec((tm, CONV_CH), lambda i: (i, 0)),
            pl.BlockSpec((HALO, CONV_CH), lambda i: (jnp.maximum(i * hb - 1, 0), 0)),
            pl.BlockSpec((HALO, CONV_CH), lambda i: (jnp.minimum((i + 1) * hb, last_halo), 0)),
            pl.BlockSpec((CONV_W + 1, CONV_CH), lambda i: (0, 0)),
            pl.BlockSpec((1, CONV_CH), lambda i: (0, 0)),
            pl.BlockSpec((1, CONV_CH), lambda i: (0, 0)),
            pl.BlockSpec((1, CONV_CH), lambda i: (0, 0)),
            pl.BlockSpec((d, d), lambda i: (0, 0)),
            pl.BlockSpec((tm, d), lambda i: (i, 0)),
            pl.BlockSpec((1, d), lambda i: (0, 0)),
        ],
        out_specs=[
            pl.BlockSpec((tm, d), lambda i: (i, 0)),
            pl.BlockSpec((tm, d), lambda i: (i, 0)),
            pl.BlockSpec((tm, ROW_TILE, LANES), lambda i: (i, 0, 0)),
        ],
        out_shape=[
            jax.ShapeDtypeStruct((t, d), jnp.float32),
            jax.ShapeDtypeStruct((t, d), jnp.float32),
            jax.ShapeDtypeStruct((t, ROW_TILE, LANES), jnp.float32),
        ],
        scratch_shapes=[
            pltpu.VMEM((tm + 2 * HALO, CONV_CH), jnp.float32),
            pltpu.VMEM((tm, CONV_CH), jnp.bfloat16),
        ],
        compiler_params=_cparams(("parallel",)),
        name="mix_out",
    )(attn, hg, hg, hg, cw, cb, lg, lb, w_out, x2, g2)


def _top_rows(s, order, payload, count):
    big = jnp.float32(2 ** 30)
    vals = []
    picks = []
    for _ in range(count):
        m = jnp.max(s, axis=0, keepdims=True)
        first = jnp.min(jnp.where(s == m, order, big), axis=0, keepdims=True)
        chosen = order == first
        if payload is order:
            pick = first
        else:
            pick = jnp.max(jnp.where(chosen, payload, -1.0), axis=0, keepdims=True)
        vals.append(m)
        picks.append(pick)
        s = jnp.where(chosen, _NEG_INF, s)
    return jnp.concatenate(vals, axis=0), jnp.concatenate(picks, axis=0)


def _candidate_blocks():
    blocks = [(0, 0, 16)]
    for a in range(1, 8):
        blocks.append((a, 0, 8))
    return blocks


def _route_kernel(hn_ref, wqt_ref, keys_ref, e_ref, g_ref, words_ref):
    hn = hn_ref[...].astype(jnp.bfloat16)
    qpt = lax.dot_general(wqt_ref[...], hn, (((1,), (1,)), ((), ())),
                          preferred_element_type=jnp.float32)
    qpt = qpt.astype(jnp.bfloat16)
    tm = hn.shape[0]
    half = PEER_DQ // 2
    words_ref[PEER_SEL // 2:, :] = jnp.zeros((LANES - PEER_SEL // 2, tm), jnp.int32)
    k0 = keys_ref[0]
    k1 = keys_ref[1]
    row_iota = lax.broadcasted_iota(jnp.int32, (N_KEYS, LANES), 0).astype(jnp.float32)

    blocks = _candidate_blocks()
    flat_parts = []
    for a, b0, nb in blocks:
        flat_parts.append(a * PEER_TOPK + b0 + lax.broadcasted_iota(jnp.int32, (nb, LANES), 0))
    tail_flat = (8 + lax.broadcasted_iota(jnp.int32, (8, LANES), 0)) * PEER_TOPK
    flat = jnp.concatenate(flat_parts + [tail_flat], axis=0).astype(jnp.float32)

    for h in range(PEER_HEADS):
        q1 = qpt[h * PEER_DQ:h * PEER_DQ + half, :]
        q2 = qpt[h * PEER_DQ + half:(h + 1) * PEER_DQ, :]
        s1 = jnp.dot(k0, q1, preferred_element_type=jnp.float32)
        s2 = jnp.dot(k1, q2, preferred_element_type=jnp.float32)
        for c in range(tm // LANES):
            sl = slice(c * LANES, (c + 1) * LANES)
            v1, i1 = _top_rows(s1[:, sl], row_iota, row_iota, PEER_TOPK)
            v2, i2 = _top_rows(s2[:, sl], row_iota, row_iota, PEER_TOPK)
            cs = []
            ce = []
            for a, b0, nb in blocks:
                cs.append(v1[a:a + 1, :] + v2[b0:b0 + nb, :])
                ce.append(i1[a:a + 1, :] * N_KEYS + i2[b0:b0 + nb, :])
            cs.append(v1[8:16, :] + v2[0:1, :])
            ce.append(i1[8:16, :] * N_KEYS + i2[0:1, :])
            cand_s = jnp.concatenate(cs, axis=0)
            cand_e = jnp.concatenate(ce, axis=0)
            top_s, top_e = _top_rows(cand_s, flat, cand_e, PEER_TOPK)
            ex = jnp.exp(top_s - top_s[0:1, :])
            gates = ex / jnp.sum(ex, axis=0, keepdims=True)
            off = top_e.astype(jnp.int32) * ROW_WORDS
            half_k = PEER_TOPK // 2
            words_ref[h * half_k:(h + 1) * half_k, sl] = (off[half_k:, :] << 16) | off[:half_k, :]
            g_ref[h, :, sl] = gates
    e_ref[...] = words_ref[...].T


def _route(hn, wqt, keys, tm):
    t, d = hn.shape
    return pl.pallas_call(
        _route_kernel,
        grid=(t // tm,),
        in_specs=[
            pl.BlockSpec((tm, d), lambda i: (i, 0)),
            pl.BlockSpec(wqt.shape, lambda i: (0, 0)),
            pl.BlockSpec(keys.shape, lambda i: (0, 0, 0)),
        ],
        out_specs=[
            pl.BlockSpec((tm, LANES), lambda i: (i, 0)),
            pl.BlockSpec((PEER_HEADS, PEER_TOPK, tm), lambda i: (0, 0, i)),
        ],
        out_shape=[
            jax.ShapeDtypeStruct((t, LANES), jnp.int32),
            jax.ShapeDtypeStruct((PEER_HEADS, PEER_TOPK, t), jnp.float32),
        ],
        scratch_shapes=[pltpu.VMEM((LANES, tm), jnp.int32)],
        compiler_params=_cparams(("parallel",)),
        name="route",
    )(hn, wqt, keys)


PEER_TOK = 128


def _expert_row(tbl_ref, off):
    words = tbl_ref[pl.ds(pl.multiple_of(off, ROW_WORDS), ROW_WORDS), :]
    return pltpu.bitcast(words, jnp.bfloat16).astype(jnp.float32)


def _expert_offsets(idx_row, head):
    lo, hi = [], []
    for j in range(PEER_TOPK // 2):
        word = idx_row[head * (PEER_TOPK // 2) + j]
        lo.append(word & 0xFFFF)
        hi.append(lax.shift_right_logical(word, 16))
    return lo + hi


def _gelu_tanh(x):
    c = np.float32(np.sqrt(2.0 / np.pi))
    return 0.5 * x * (1.0 + jnp.tanh(c * (x + 0.044715 * (x * x * x))))


def _fold_rows(rows, x, x_swapped, masks):
    lo4, mod4_lo2, even = masks
    a, e, c, g, b, f, d, h = rows

    def level1(p, q):
        t1 = jnp.where(lo4, p, q)
        t2 = pltpu.roll(jnp.where(lo4, q, p), 4, axis=0)
        return t1 * x + t2 * x_swapped

    def level2(u, v):
        uu = u + pltpu.roll(u, 6, axis=0)
        vv = v + pltpu.roll(v, 2, axis=0)
        return jnp.where(mod4_lo2, uu, vv)

    def level3(u, v):
        uu = u + pltpu.roll(u, 7, axis=0)
        vv = v + pltpu.roll(v, 1, axis=0)
        return jnp.where(even, uu, vv)

    return level3(level2(level1(a, b), level1(c, d)), level2(level1(e, f), level1(g, h)))


U_TOKENS_PER_STEP = 16


def _peer_u_kernel(e_ref, x_ref, gate_ref, tbl_ref, w_ref, s_ref, part_a, part_b):
    lane = lax.broadcasted_iota(jnp.int32, (PEER_SEL, PEER_TOK), 1)
    sub = lax.broadcasted_iota(jnp.int32, (SUBLANES, LANES), 0)
    masks = (sub < 4, (sub % 4) < 2, (sub % 2) == 0)

    def fold(t, part_ref):
        e_row = e_ref.at[t]
        x = x_ref[t]
        x_swapped = pltpu.roll(x, 4, axis=0)
        for head in range(PEER_HEADS):
            offs = _expert_offsets(e_row, head)
            for half in range(PEER_TOPK // SUBLANES):
                grp = head * (PEER_TOPK // SUBLANES) + half
                rows = [_expert_row(tbl_ref, offs[half * SUBLANES + j]) for j in range(SUBLANES)]
                part_ref[grp * SUBLANES:(grp + 1) * SUBLANES, :] = _fold_rows(rows, x, x_swapped, masks)

    def finish(t, part_ref):
        s = jnp.sum(part_ref[...], axis=1, keepdims=True)
        s_ref[...] = jnp.where(lane == t, s, s_ref[...])

    s_ref[...] = jnp.zeros(s_ref.shape, jnp.float32)
    part_b[...] = jnp.zeros(part_b.shape, jnp.float32)

    def token_block(i, carry):
        t0 = U_TOKENS_PER_STEP * i
        for j in range(U_TOKENS_PER_STEP):
            cur, prev = (part_a, part_b) if j % 2 == 0 else (part_b, part_a)
            fold(t0 + j, cur)
            finish(t0 + j - 1, prev)
        return carry

    lax.fori_loop(0, PEER_TOK // U_TOKENS_PER_STEP, token_block, 0)
    finish(PEER_TOK - 1, part_b)
    gates = gate_ref[...].reshape(PEER_SEL, PEER_TOK)
    w_ref[...] = gates * _gelu_tanh(s_ref[...])


def _peer_u(e_sm, x3, gates, tbl):
    t = x3.shape[0]
    return pl.pallas_call(
        _peer_u_kernel,
        grid=(t // PEER_TOK,),
        in_specs=[
            pl.BlockSpec((PEER_TOK, LANES), lambda i: (i, 0), memory_space=pltpu.SMEM),
            pl.BlockSpec((PEER_TOK, SUBLANES, LANES), lambda i: (i, 0, 0)),
            pl.BlockSpec((PEER_HEADS, PEER_TOPK, PEER_TOK), lambda i: (0, 0, i)),
            pl.BlockSpec(tbl.shape, lambda i: (0, 0), pipeline_mode=pl.Buffered(1)),
        ],
        out_specs=pl.BlockSpec((PEER_SEL, PEER_TOK), lambda i: (0, i)),
        out_shape=jax.ShapeDtypeStruct((PEER_SEL, t), jnp.float32),
        scratch_shapes=[
            pltpu.VMEM((PEER_SEL, PEER_TOK), jnp.float32),
            pltpu.VMEM((PEER_SEL, LANES), jnp.float32),
            pltpu.VMEM((PEER_SEL, LANES), jnp.float32),
        ],
        compiler_params=_cparams(("arbitrary",)),
        name="peer_u",
    )(e_sm, x3, gates, tbl)


V_TOKENS_PER_STEP = SUBLANES


def _peer_v_kernel(e_ref, wt_ref, x1_ref, fg_ref, tbl_ref, y_ref):
    lane = lax.broadcasted_iota(jnp.int32, (PEER_SEL, PEER_TOK), 1)

    def spread(t):
        col = jnp.sum(jnp.where(lane == t, wt_ref[...], 0.0), axis=1, keepdims=True)
        return jnp.broadcast_to(col, (PEER_SEL, LANES))

    def weighted_sum(t, wb, acc):
        e_row = e_ref.at[t]
        for head in range(PEER_HEADS):
            offs = _expert_offsets(e_row, head)
            for half in range(PEER_TOPK // SUBLANES):
                terms = []
                for j in range(SUBLANES):
                    slot = half * SUBLANES + j
                    k = head * PEER_TOPK + slot
                    terms.append(wb[k:k + 1, :] * _expert_row(tbl_ref, offs[slot]))
                while len(terms) > 1:
                    terms = [terms[i] + terms[i + 1] for i in range(0, len(terms), 2)]
                acc = acc + terms[0]
        return acc

    def token_group(i, wb):
        t0 = pl.multiple_of(i * V_TOKENS_PER_STEP, V_TOKENS_PER_STEP)
        x8 = x1_ref[pl.ds(t0, V_TOKENS_PER_STEP), :]
        tiles = []
        for j in range(V_TOKENS_PER_STEP):
            wb_next = spread(jnp.minimum(t0 + j + 1, PEER_TOK - 1))
            resid = jnp.concatenate([x8[j:j + 1, c * LANES:(c + 1) * LANES] for c in TILE_CHUNK], axis=0)
            tiles.append(weighted_sum(t0 + j, wb, resid))
            wb = wb_next
        for r, c in enumerate(TILE_CHUNK):
            rows = jnp.concatenate([tile[r:r + 1, :] for tile in tiles], axis=0)
            y_ref[pl.ds(t0, V_TOKENS_PER_STEP), c * LANES:(c + 1) * LANES] = rows
        return wb

    lax.fori_loop(0, PEER_TOK // V_TOKENS_PER_STEP, token_group, spread(0))
    x2 = y_ref[...]
    ms = jnp.mean(x2 * x2, axis=-1, keepdims=True)
    y_ref[...] = x2 * lax.rsqrt(ms + EPS) * fg_ref[...]


def _peer_v(e_sm, wt, x13, fg, tbl):
    t = x13.shape[0]
    return pl.pallas_call(
        _peer_v_kernel,
        grid=(t // PEER_TOK,),
        in_specs=[
            pl.BlockSpec((PEER_TOK, LANES), lambda i: (i, 0), memory_space=pltpu.SMEM),
            pl.BlockSpec((PEER_SEL, PEER_TOK), lambda i: (0, i)),
            pl.BlockSpec((PEER_TOK, ROW_TILE * LANES), lambda i: (i, 0)),
            pl.BlockSpec((1, ROW_TILE * LANES), lambda i: (0, 0)),
            pl.BlockSpec(tbl.shape, lambda i: (0, 0), pipeline_mode=pl.Buffered(1)),
        ],
        out_specs=pl.BlockSpec((PEER_TOK, ROW_TILE * LANES), lambda i: (i, 0)),
        out_shape=jax.ShapeDtypeStruct((t, ROW_TILE * LANES), jnp.float32),
        compiler_params=_cparams(("arbitrary",)),
        name="peer_v",
    )(e_sm, wt, x13, fg, tbl)


PACK_ROWS = 512


def _pack_kernel(tbl_ref, out_ref):
    half = ROW_WORDS * LANES
    for p in range(ROW_WORDS):
        lo = tbl_ref[:, p * LANES:(p + 1) * LANES]
        hi = tbl_ref[:, half + p * LANES:half + (p + 1) * LANES]
        words = pltpu.pack_elementwise([lo, hi], packed_dtype=jnp.bfloat16)
        out_ref[pl.ds(p, PACK_ROWS, stride=ROW_WORDS), :] = pltpu.bitcast(words, jnp.int32)


def _pack_table(tbl):
    n, d = tbl.shape
    assert d == ROW_TILE * LANES and n % PACK_ROWS == 0
    return pl.pallas_call(
        _pack_kernel,
        grid=(n // PACK_ROWS,),
        in_specs=[pl.BlockSpec((PACK_ROWS, d), lambda i: (i, 0))],
        out_specs=pl.BlockSpec((PACK_ROWS * ROW_WORDS, LANES), lambda i: (i, 0)),
        out_shape=jax.ShapeDtypeStruct((n * ROW_WORDS, LANES), jnp.int32),
        compiler_params=_cparams(("parallel",)),
        name="pack_table",
    )(tbl)


def _rope_tables(seq):
    pos = jnp.arange(seq)
    row = (pos // GRID_W).astype(jnp.float32)
    col = (pos % GRID_W).astype(jnp.float32)
    n_pairs = ROPE_AXIS_DIM // 2
    inv_freq = ROPE_THETA ** (-jnp.arange(n_pairs, dtype=jnp.float32) / n_pairs)
    ang_r = row[:, None] * inv_freq[None, :]
    ang_c = col[:, None] * inv_freq[None, :]
    cos = jnp.concatenate([jnp.cos(ang_r), jnp.cos(ang_r), jnp.cos(ang_c), jnp.cos(ang_c)], axis=1)
    sin = jnp.concatenate([-jnp.sin(ang_r), jnp.sin(ang_r), -jnp.sin(ang_c), jnp.sin(ang_c)], axis=1)
    reps = LANES // HEAD_DIM
    return jnp.tile(cos, (1, reps)), jnp.tile(sin, (1, reps))


def kernel(x, norm1_g, w_in, q_norm_g, k_norm_g, conv_dw, conv_b, conv_ln_g, conv_ln_b, w_out,
           norm2_g, peer_wq, peer_keys, peer_u, peer_v, final_g):
    b, s, d = x.shape
    t = b * s
    assert d == SUBLANES * LANES and s % GRID_W == 0 and N_KV_HEADS == 2
    tm = min(512, s)
    tq = min(512, s)
    tk = min(1024, s)
    tr = min(256, s)
    assert s % tm == 0 and t % PEER_TOK == 0

    f32 = jnp.float32
    bf16 = jnp.bfloat16
    x2 = x.reshape(t, d)
    cos_t, sin_t = _rope_tables(s)
    head_id = jnp.arange(ATTN_WIDTH) // HEAD_DIM
    ones_bd = (head_id[:, None] == head_id[None, :]).astype(bf16)

    qt, k, vt, hg = _in_proj(
        x2, norm1_g.reshape(1, d), w_in.astype(bf16),
        jnp.tile(q_norm_g, N_HEADS).reshape(1, ATTN_WIDTH),
        jnp.tile(k_norm_g, N_KV_HEADS).reshape(1, KV_WIDTH),
        cos_t, sin_t, ones_bd, b, s, tm)

    attn = _attention(qt, k, vt, b, s, tq, tk)

    cw = jnp.concatenate([conv_dw.reshape(CONV_W, CONV_CH), jnp.zeros((1, CONV_CH), f32)], axis=0)
    x1, hn, hnt = _mix_out(attn, hg, cw, conv_b.reshape(1, CONV_CH), conv_ln_g.reshape(1, CONV_CH),
                      conv_ln_b.reshape(1, CONV_CH), w_out.astype(bf16), x2,
                      norm2_g.reshape(1, d), s, tm)

    e_sm, gates = _route(hn, peer_wq.T.astype(bf16), peer_keys.astype(bf16), tr)

    w = _peer_u(e_sm, hnt, gates, _pack_table(peer_u))
    y = _peer_v(e_sm, w, x1, final_g.reshape(1, d), _pack_table(peer_v))
    return y.reshape(b, s, d)
```

```python
import functools

import jax
import jax.numpy as jnp
import numpy as np
from jax import lax
from jax.experimental import pallas as pl
from jax.experimental.pallas import tpu as pltpu

HEAD_DIM = 64
N_HEADS = 8
N_KV_HEADS = 2
GROUP = N_HEADS // N_KV_HEADS
ATTN_WIDTH = N_HEADS * HEAD_DIM
KV_WIDTH = N_KV_HEADS * HEAD_DIM
CONV_CH = 512
CONV_W = 31
CONV_PAD = (CONV_W - 1) // 2
GRID_W = 64
ROPE_THETA = 10000.0
ROPE_AXIS_DIM = HEAD_DIM // 2
N_KEYS = 128
PEER_HEADS = 8
PEER_DQ = 256
PEER_TOPK = 16
PEER_SEL = PEER_HEADS * PEER_TOPK
EPS = 1e-6

LANES = 128
SUBLANES = 8
HALO = 16
BF16_ROWS = 16
V_EXT = HEAD_DIM + BF16_ROWS
LOG2_E = 1.4426950408889634
VMEM_LIMIT = 56 * 1024 * 1024

ROW_TILE = 8
ROW_WORDS = ROW_TILE // 2
TILE_CHUNK = tuple((r % 2) * ROW_WORDS + r // 2 for r in range(ROW_TILE))

_NEG_INF = float("-inf")


def _cparams(sem, vmem=VMEM_LIMIT):
    return pltpu.CompilerParams(dimension_semantics=sem, vmem_limit_bytes=vmem)


def _group_mean_sq(v, ones_blockdiag):
    sq = v * v
    hi = sq.astype(jnp.bfloat16)
    lo = (sq - hi.astype(jnp.float32)).astype(jnp.bfloat16)
    tot = (jnp.dot(hi, ones_blockdiag, preferred_element_type=jnp.float32)
           + jnp.dot(lo, ones_blockdiag, preferred_element_type=jnp.float32))
    return tot * (1.0 / HEAD_DIM)


def _rope(v, cos, sin_signed, lane_in_pair_lo):
    width = v.shape[-1]
    up = pltpu.roll(v, width - ROPE_AXIS_DIM // 2, axis=1)
    down = pltpu.roll(v, ROPE_AXIS_DIM // 2, axis=1)
    partner = jnp.where(lane_in_pair_lo, up, down)
    return v * cos + partner * sin_signed


def _in_proj_kernel(x_ref, g1_ref, w_ref, qg_ref, kg_ref, cos_ref, sin_ref, ones_ref,
                    qt_ref, k_ref, vt_ref, hg_ref):
    x = x_ref[...]
    ms = jnp.mean(x * x, axis=-1, keepdims=True)
    h = (x * lax.rsqrt(ms + EPS) * g1_ref[...]).astype(jnp.bfloat16)
    p = jnp.dot(h, w_ref[...], preferred_element_type=jnp.float32)
    o1 = ATTN_WIDTH
    o2 = o1 + KV_WIDTH
    o3 = o2 + KV_WIDTH
    o4 = o3 + CONV_CH
    q = p[:, :o1]
    k = p[:, o1:o2]
    v = p[:, o2:o3]
    a = p[:, o3:o4]
    gate = p[:, o4:]

    ones_bd = ones_ref[...]
    qn = q * lax.rsqrt(_group_mean_sq(q, ones_bd) + EPS) * qg_ref[...]
    kn = k * lax.rsqrt(_group_mean_sq(k, ones_bd[:KV_WIDTH, :KV_WIDTH]) + EPS) * kg_ref[...]

    cos2 = cos_ref[...]
    sin2 = sin_ref[...]
    cos_q = jnp.concatenate([cos2] * (ATTN_WIDTH // LANES), axis=1)
    sin_q = jnp.concatenate([sin2] * (ATTN_WIDTH // LANES), axis=1)
    lane_q = lax.broadcasted_iota(jnp.int32, q.shape, 1)
    lane_k = lax.broadcasted_iota(jnp.int32, k.shape, 1)
    half = ROPE_AXIS_DIM // 2
    qr = _rope(qn, cos_q, sin_q, (lane_q % ROPE_AXIS_DIM) < half) * (HEAD_DIM ** -0.5 * LOG2_E)
    kr = _rope(kn, cos2, sin2, (lane_k % ROPE_AXIS_DIM) < half)

    qt = qr.T.astype(qt_ref.dtype)
    zeros = jnp.zeros((HEAD_DIM, qt.shape[1]), qt_ref.dtype)
    for hh in range(N_HEADS):
        g = hh // GROUP
        piece = qt[hh * HEAD_DIM:(hh + 1) * HEAD_DIM, :]
        qt_ref[0, hh, g * HEAD_DIM:(g + 1) * HEAD_DIM, :] = piece
        qt_ref[0, hh, (1 - g) * HEAD_DIM:(2 - g) * HEAD_DIM, :] = zeros
    k_ref[...] = kr.astype(k_ref.dtype)
    vt = v.T.astype(vt_ref.dtype)
    pad_row = lax.broadcasted_iota(jnp.int32, (V_EXT - HEAD_DIM, vt.shape[1]), 0)
    ones_pad = jnp.where(pad_row == 0, 1.0, 0.0).astype(vt_ref.dtype)
    for g in range(N_KV_HEADS):
        vt_ref[0, g, 0:HEAD_DIM, :] = vt[g * HEAD_DIM:(g + 1) * HEAD_DIM, :]
        vt_ref[0, g, HEAD_DIM:V_EXT, :] = ones_pad
    hg_ref[...] = a * (1.0 / (1.0 + jnp.exp(-gate)))


def _in_proj(x2, g1, w_in, qg, kg, cos_t, sin_t, ones_bd, batch, seq, tm):
    t, d = x2.shape
    n_s = seq // tm
    in_width = w_in.shape[1]
    return pl.pallas_call(
        _in_proj_kernel,
        grid=(t // tm,),
        in_specs=[
            pl.BlockSpec((tm, d), lambda i: (i, 0)),
            pl.BlockSpec((1, d), lambda i: (0, 0)),
            pl.BlockSpec((d, in_width), lambda i: (0, 0)),
            pl.BlockSpec((1, ATTN_WIDTH), lambda i: (0, 0)),
            pl.BlockSpec((1, KV_WIDTH), lambda i: (0, 0)),
            pl.BlockSpec((tm, LANES), lambda i: (i % n_s, 0)),
            pl.BlockSpec((tm, LANES), lambda i: (i % n_s, 0)),
            pl.BlockSpec((ATTN_WIDTH, ATTN_WIDTH), lambda i: (0, 0)),
        ],
        out_specs=[
            pl.BlockSpec((1, N_HEADS, KV_WIDTH, tm), lambda i: (i // n_s, 0, 0, i % n_s)),
            pl.BlockSpec((tm, KV_WIDTH), lambda i: (i, 0)),
            pl.BlockSpec((1, N_KV_HEADS, V_EXT, tm), lambda i: (i // n_s, 0, 0, i % n_s)),
            pl.BlockSpec((tm, CONV_CH), lambda i: (i, 0)),
        ],
        out_shape=[
            jax.ShapeDtypeStruct((batch, N_HEADS, KV_WIDTH, seq), jnp.bfloat16),
            jax.ShapeDtypeStruct((t, KV_WIDTH), jnp.bfloat16),
            jax.ShapeDtypeStruct((batch, N_KV_HEADS, V_EXT, seq), jnp.bfloat16),
            jax.ShapeDtypeStruct((t, CONV_CH), jnp.float32),
        ],
        compiler_params=_cparams(("parallel",)),
        name="in_proj",
    )(x2, g1, w_in, qg, kg, cos_t, sin_t, ones_bd)


SCORES_AHEAD = 2


def _attn_kernel(qt_ref, k_ref, vt_ref, o_ref, m_ref, acc_ref):
    ki = pl.program_id(2)

    @pl.when(ki == 0)
    def _():
        m_ref[...] = jnp.full(m_ref.shape, _NEG_INF, jnp.float32)
        acc_ref[...] = jnp.zeros(acc_ref.shape, jnp.float32)

    k = k_ref[...]

    def scores(h):
        return jnp.dot(k, qt_ref[0, h], preferred_element_type=jnp.float32)

    ahead = [scores(h) for h in range(SCORES_AHEAD)]
    for h in range(N_HEADS):
        g = h // GROUP
        s = ahead.pop(0)
        if h + SCORES_AHEAD < N_HEADS:
            ahead.append(scores(h + SCORES_AHEAD))
        m_prev = m_ref[h:h + 1, :]
        m_new = jnp.maximum(m_prev, jnp.max(s, axis=0, keepdims=True))
        alpha = jnp.exp2(m_prev - m_new)
        p = jnp.exp2(s - m_new).astype(jnp.bfloat16)
        pv = jnp.dot(vt_ref[0, g], p, preferred_element_type=jnp.float32)
        acc_ref[h] = alpha * acc_ref[h] + pv
        m_ref[h:h + 1, :] = m_new

    @pl.when(ki == pl.num_programs(2) - 1)
    def _():
        outs = []
        for h in range(N_HEADS):
            acc = acc_ref[h]
            outs.append(acc[:HEAD_DIM, :] / acc[HEAD_DIM:HEAD_DIM + 1, :])
        o_ref[...] = jnp.concatenate(outs, axis=0).T.astype(o_ref.dtype)


def _attention(qt, k, vt, batch, seq, tq, tk):
    t = k.shape[0]
    nq = seq // tq
    nk = seq // tk
    return pl.pallas_call(
        _attn_kernel,
        grid=(batch, nq, nk),
        in_specs=[
            pl.BlockSpec((1, N_HEADS, KV_WIDTH, tq), lambda b, qi, ki: (b, 0, 0, qi)),
            pl.BlockSpec((tk, KV_WIDTH), lambda b, qi, ki: (b * nk + ki, 0)),
            pl.BlockSpec((1, N_KV_HEADS, V_EXT, tk), lambda b, qi, ki: (b, 0, 0, ki)),
        ],
        out_specs=pl.BlockSpec((tq, ATTN_WIDTH), lambda b, qi, ki: (b * nq + qi, 0)),
        out_shape=jax.ShapeDtypeStruct((t, ATTN_WIDTH), jnp.bfloat16),
        scratch_shapes=[
            pltpu.VMEM((N_HEADS, tq), jnp.float32),
            pltpu.VMEM((N_HEADS, V_EXT, tq), jnp.float32),
        ],
        compiler_params=_cparams(("parallel", "parallel", "arbitrary")),
        name="attention",
    )(qt, k, vt)


CONV_ROWS = 32


def _mix_out_kernel(attn_ref, hg_ref, prev_ref, next_ref, cw_ref, cb_ref, lg_ref, lb_ref,
                    wo_ref, x_ref, g2_ref, x1_ref, hn_ref, hnt_ref, ext_ref, conv_ref, *, n_s):
    tm = hg_ref.shape[0]
    si = pl.program_id(0) % n_s
    keep_prev = (si > 0).astype(jnp.float32)
    keep_next = (si < n_s - 1).astype(jnp.float32)
    ext_ref[0:HALO, :] = prev_ref[...] * keep_prev
    ext_ref[HALO:HALO + tm, :] = hg_ref[...]
    ext_ref[HALO + tm:HALO + tm + HALO, :] = next_ref[...] * keep_next

    cw = cw_ref[...]
    cb = cb_ref[...]
    lg = lg_ref[...]
    lb = lb_ref[...]

    def conv_chunk(c):
        start = pl.multiple_of(c * CONV_ROWS, CONV_ROWS)
        rows = CONV_ROWS + 2 * HALO
        parts = []
        for lt in range(CONV_CH // LANES):
            cols = slice(lt * LANES, (lt + 1) * LANES)
            win = ext_ref[pl.ds(start, rows), cols]
            acc_l = jnp.zeros((CONV_ROWS, LANES), jnp.float32)
            for shift in range(SUBLANES):
                shifted = win if shift == 0 else pltpu.roll(win, rows - shift, axis=0)
                for j in range(CONV_W):
                    off = HALO - CONV_PAD + j
                    if off % SUBLANES == shift:
                        base = off - shift
                        acc_l = acc_l + shifted[base:base + CONV_ROWS, :] * cw[j:j + 1, cols]
            parts.append(acc_l)
        acc = jnp.concatenate(parts, axis=1)
        acc = acc + cb
        mu = jnp.mean(acc, axis=-1, keepdims=True)
        cen = acc - mu
        var = jnp.mean(cen * cen, axis=-1, keepdims=True)
        y = cen * lax.rsqrt(var + EPS) * lg + lb
        y = y * (1.0 / (1.0 + jnp.exp(-y)))
        conv_ref[pl.ds(start, CONV_ROWS), :] = y.astype(conv_ref.dtype)

    def conv_step(c, carry):
        conv_chunk(2 * c)
        conv_chunk(2 * c + 1)
        return carry

    lax.fori_loop(0, tm // (2 * CONV_ROWS), conv_step, 0)

    mixed = (jnp.dot(attn_ref[...], wo_ref[0:ATTN_WIDTH, :], preferred_element_type=jnp.float32)
             + jnp.dot(conv_ref[...], wo_ref[ATTN_WIDTH:, :], preferred_element_type=jnp.float32))
    x1 = x_ref[...] + mixed
    x1_ref[...] = x1
    ms = jnp.mean(x1 * x1, axis=-1, keepdims=True)
    hn = x1 * lax.rsqrt(ms + EPS) * g2_ref[...]
    hn_ref[...] = hn
    for r, c in enumerate(TILE_CHUNK):
        hnt_ref[:, r, :] = hn[:, c * LANES:(c + 1) * LANES]


def _mix_out(attn, hg, cw, cb, lg, lb, w_out, x2, g2, seq, tm):
    t, d = x2.shape
    n_s = seq // tm
    hb = tm // HALO
    last_halo = t // HALO - 1
    return pl.pallas_call(
        functools.partial(_mix_out_kernel, n_s=n_s),
        grid=(t // tm,),
        in_specs=[
            pl.BlockSpec((tm, ATTN_WIDTH), lambda i: (i, 0)),
            pl.BlockSpec((tm, CONV_CH), lambda i: (i, 0)),
            pl.BlockSpec((HALO, CONV_CH), lambda i: (jnp.maximum(i * hb - 1, 0), 0)),
            pl.BlockSpec((HALO, CONV_CH), lambda i: (jnp.minimum((i + 1) * hb, last_halo), 0)),
            pl.BlockSpec((CONV_W + 1, CONV_CH), lambda i: (0, 0)),
            pl.BlockSpec((1, CONV_CH), lambda i: (0, 0)),
            pl.BlockSpec((1, CONV_CH), lambda i: (0, 0)),
            pl.BlockSpec((1, CONV_CH), lambda i: (0, 0)),
            pl.BlockSpec((d, d), lambda i: (0, 0)),
            pl.BlockSpec((tm, d), lambda i: (i, 0)),
            pl.BlockSpec((1, d), lambda i: (0, 0)),
        ],
        out_specs=[
            pl.BlockSpec((tm, d), lambda i: (i, 0)),
            pl.BlockSpec((tm, d), lambda i: (i, 0)),
            pl.BlockSpec((tm, ROW_TILE, LANES), lambda i: (i, 0, 0)),
        ],
        out_shape=[
            jax.ShapeDtypeStruct((t, d), jnp.float32),
            jax.ShapeDtypeStruct((t, d), jnp.float32),
            jax.ShapeDtypeStruct((t, ROW_TILE, LANES), jnp.float32),
        ],
        scratch_shapes=[
            pltpu.VMEM((tm + 2 * HALO, CONV_CH), jnp.float32),
            pltpu.VMEM((tm, CONV_CH), jnp.bfloat16),
        ],
        compiler_params=_cparams(("parallel",)),
        name="mix_out",
    )(attn, hg, hg, hg, cw, cb, lg, lb, w_out, x2, g2)


def _top_rows(s, order, payload, count):
    big = jnp.float32(2 ** 30)
    vals = []
    picks = []
    for _ in range(count):
        m = jnp.max(s, axis=0, keepdims=True)
        first = jnp.min(jnp.where(s == m, order, big), axis=0, keepdims=True)
        chosen = order == first
        if payload is order:
            pick = first
        else:
            pick = jnp.max(jnp.where(chosen, payload, -1.0), axis=0, keepdims=True)
        vals.append(m)
        picks.append(pick)
        s = jnp.where(chosen, _NEG_INF, s)
    return jnp.concatenate(vals, axis=0), jnp.concatenate(picks, axis=0)


def _candidate_blocks():
    blocks = [(0, 0, 16)]
    for a in range(1, 8):
        blocks.append((a, 0, 8))
    return blocks


def _route_kernel(hn_ref, wqt_ref, keys_ref, e_ref, g_ref, words_ref):
    hn = hn_ref[...].astype(jnp.bfloat16)
    qpt = lax.dot_general(wqt_ref[...], hn, (((1,), (1,)), ((), ())),
                          preferred_element_type=jnp.float32)
    qpt = qpt.astype(jnp.bfloat16)
    tm = hn.shape[0]
    half = PEER_DQ // 2
    words_ref[PEER_SEL // 2:, :] = jnp.zeros((LANES - PEER_SEL // 2, tm), jnp.int32)
    k0 = keys_ref[0]
    k1 = keys_ref[1]
    row_iota = lax.broadcasted_iota(jnp.int32, (N_KEYS, LANES), 0).astype(jnp.float32)

    blocks = _candidate_blocks()
    flat_parts = []
    for a, b0, nb in blocks:
        flat_parts.append(a * PEER_TOPK + b0 + lax.broadcasted_iota(jnp.int32, (nb, LANES), 0))
    tail_flat = (8 + lax.broadcasted_iota(jnp.int32, (8, LANES), 0)) * PEER_TOPK
    flat = jnp.concatenate(flat_parts + [tail_flat], axis=0).astype(jnp.float32)

    for h in range(PEER_HEADS):
        q1 = qpt[h * PEER_DQ:h * PEER_DQ + half, :]
        q2 = qpt[h * PEER_DQ + half:(h + 1) * PEER_DQ, :]
        s1 = jnp.dot(k0, q1, preferred_element_type=jnp.float32)
        s2 = jnp.dot(k1, q2, preferred_element_type=jnp.float32)
        for c in range(tm // LANES):
            sl = slice(c * LANES, (c + 1) * LANES)
            v1, i1 = _top_rows(s1[:, sl], row_iota, row_iota, PEER_TOPK)
            v2, i2 = _top_rows(s2[:, sl], row_iota, row_iota, PEER_TOPK)
            cs = []
            ce = []
            for a, b0, nb in blocks:
                cs.append(v1[a:a + 1, :] + v2[b0:b0 + nb, :])
                ce.append(i1[a:a + 1, :] * N_KEYS + i2[b0:b0 + nb, :])
            cs.append(v1[8:16, :] + v2[0:1, :])
            ce.append(i1[8:16, :] * N_KEYS + i2[0:1, :])
            cand_s = jnp.concatenate(cs, axis=0)
            cand_e = jnp.concatenate(ce, axis=0)
            top_s, top_e = _top_rows(cand_s, flat, cand_e, PEER_TOPK)
            ex = jnp.exp(top_s - top_s[0:1, :])
            gates = ex / jnp.sum(ex, axis=0, keepdims=True)
            off = top_e.astype(jnp.int32) * ROW_WORDS
            half_k = PEER_TOPK // 2
            words_ref[h * half_k:(h + 1) * half_k, sl] = (off[half_k:, :] << 16) | off[:half_k, :]
            g_ref[h, :, sl] = gates
    e_ref[...] = words_ref[...].T


def _route(hn, wqt, keys, tm):
    t, d = hn.shape
    return pl.pallas_call(
        _route_kernel,
        grid=(t // tm,),
        in_specs=[
            pl.BlockSpec((tm, d), lambda i: (i, 0)),
            pl.BlockSpec(wqt.shape, lambda i: (0, 0)),
            pl.BlockSpec(keys.shape, lambda i: (0, 0, 0)),
        ],
        out_specs=[
            pl.BlockSpec((tm, LANES), lambda i: (i, 0)),
            pl.BlockSpec((PEER_HEADS, PEER_TOPK, tm), lambda i: (0, 0, i)),
        ],
        out_shape=[
            jax.ShapeDtypeStruct((t, LANES), jnp.int32),
            jax.ShapeDtypeStruct((PEER_HEADS, PEER_TOPK, t), jnp.float32),
        ],
        scratch_shapes=[pltpu.VMEM((LANES, tm), jnp.int32)],
        compiler_params=_cparams(("parallel",)),
        name="route",
    )(hn, wqt, keys)


PEER_TOK = 128


def _expert_row(tbl_ref, off):
    words = tbl_ref[pl.ds(pl.multiple_of(off, ROW_WORDS), ROW_WORDS), :]
    return pltpu.bitcast(words, jnp.bfloat16).astype(jnp.float32)


def _expert_offsets(idx_row, head):
    lo, hi = [], []
    for j in range(PEER_TOPK // 2):
        word = idx_row[head * (PEER_TOPK // 2) + j]
        lo.append(word & 0xFFFF)
        hi.append(lax.shift_right_logical(word, 16))
    return lo + hi


def _gelu_tanh(x):
    c = np.float32(np.sqrt(2.0 / np.pi))
    return 0.5 * x * (1.0 + jnp.tanh(c * (x + 0.044715 * (x * x * x))))


def _fold_rows(rows, x, x_swapped, masks):
    lo4, mod4_lo2, even = masks
    a, e, c, g, b, f, d, h = rows

    def level1(p, q):
        t1 = jnp.where(lo4, p, q)
        t2 = pltpu.roll(jnp.where(lo4, q, p), 4, axis=0)
        return t1 * x + t2 * x_swapped

    def level2(u, v):
        uu = u + pltpu.roll(u, 6, axis=0)
        vv = v + pltpu.roll(v, 2, axis=0)
        return jnp.where(mod4_lo2, uu, vv)

    def level3(u, v):
        uu = u + pltpu.roll(u, 7, axis=0)
        vv = v + pltpu.roll(v, 1, axis=0)
        return jnp.where(even, uu, vv)

    return level3(level2(level1(a, b), level1(c, d)), level2(level1(e, f), level1(g, h)))


U_TOKENS_PER_STEP = 16


def _peer_u_kernel(e_ref, x_ref, gate_ref, tbl_ref, w_ref, s_ref, part_a, part_b):
    lane = lax.broadcasted_iota(jnp.int32, (PEER_SEL, PEER_TOK), 1)
    sub = lax.broadcasted_iota(jnp.int32, (SUBLANES, LANES), 0)
    masks = (sub < 4, (sub % 4) < 2, (sub % 2) == 0)

    def fold(t, part_ref):
        e_row = e_ref.at[t]
        x = x_ref[t]
        x_swapped = pltpu.roll(x, 4, axis=0)
        for head in range(PEER_HEADS):
            offs = _expert_offsets(e_row, head)
            for half in range(PEER_TOPK // SUBLANES):
                grp = head * (PEER_TOPK // SUBLANES) + half
                rows = [_expert_row(tbl_ref, offs[half * SUBLANES + j]) for j in range(SUBLANES)]
                part_ref[grp * SUBLANES:(grp + 1) * SUBLANES, :] = _fold_rows(rows, x, x_swapped, masks)

    def finish(t, part_ref):
        s = jnp.sum(part_ref[...], axis=1, keepdims=True)
        s_ref[...] = jnp.where(lane == t, s, s_ref[...])

    s_ref[...] = jnp.zeros(s_ref.shape, jnp.float32)
    part_b[...] = jnp.zeros(part_b.shape, jnp.float32)

    def token_block(i, carry):
        t0 = U_TOKENS_PER_STEP * i
        for j in range(U_TOKENS_PER_STEP):
            cur, prev = (part_a, part_b) if j % 2 == 0 else (part_b, part_a)
            fold(t0 + j, cur)
            finish(t0 + j - 1, prev)
        return carry

    lax.fori_loop(0, PEER_TOK // U_TOKENS_PER_STEP, token_block, 0)
    finish(PEER_TOK - 1, part_b)
    gates = gate_ref[...].reshape(PEER_SEL, PEER_TOK)
    w_ref[...] = gates * _gelu_tanh(s_ref[...])


def _peer_u(e_sm, x3, gates, tbl):
    t = x3.shape[0]
    return pl.pallas_call(
        _peer_u_kernel,
        grid=(t // PEER_TOK,),
        in_specs=[
            pl.BlockSpec((PEER_TOK, LANES), lambda i: (i, 0), memory_space=pltpu.SMEM),
            pl.BlockSpec((PEER_TOK, SUBLANES, LANES), lambda i: (i, 0, 0)),
            pl.BlockSpec((PEER_HEADS, PEER_TOPK, PEER_TOK), lambda i: (0, 0, i)),
            pl.BlockSpec(tbl.shape, lambda i: (0, 0), pipeline_mode=pl.Buffered(1)),
        ],
        out_specs=pl.BlockSpec((PEER_SEL, PEER_TOK), lambda i: (0, i)),
        out_shape=jax.ShapeDtypeStruct((PEER_SEL, t), jnp.float32),
        scratch_shapes=[
            pltpu.VMEM((PEER_SEL, PEER_TOK), jnp.float32),
            pltpu.VMEM((PEER_SEL, LANES), jnp.float32),
            pltpu.VMEM((PEER_SEL, LANES), jnp.float32),
        ],
        compiler_params=_cparams(("arbitrary",)),
        name="peer_u",
    )(e_sm, x3, gates, tbl)


V_TOKENS_PER_STEP = SUBLANES


def _peer_v_kernel(e_ref, wt_ref, x1_ref, fg_ref, tbl_ref, y_ref):
    lane = lax.broadcasted_iota(jnp.int32, (PEER_SEL, PEER_TOK), 1)

    def spread(t):
        col = jnp.sum(jnp.where(lane == t, wt_ref[...], 0.0), axis=1, keepdims=True)
        return jnp.broadcast_to(col, (PEER_SEL, LANES))

    def weighted_sum(t, wb, acc):
        e_row = e_ref.at[t]
        for head in range(PEER_HEADS):
            offs = _expert_offsets(e_row, head)
            for half in range(PEER_TOPK // SUBLANES):
                terms = []
                for j in range(SUBLANES):
                    slot = half * SUBLANES + j
                    k = head * PEER_TOPK + slot
                    terms.append(wb[k:k + 1, :] * _expert_row(tbl_ref, offs[slot]))
                while len(terms) > 1:
                    terms = [terms[i] + terms[i + 1] for i in range(0, len(terms), 2)]
                acc = acc + terms[0]
        return acc

    def token_group(i, wb):
        t0 = pl.multiple_of(i * V_TOKENS_PER_STEP, V_TOKENS_PER_STEP)
        x8 = x1_ref[pl.ds(t0, V_TOKENS_PER_STEP), :]
        tiles = []
        for j in range(V_TOKENS_PER_STEP):
            wb_next = spread(jnp.minimum(t0 + j + 1, PEER_TOK - 1))
            resid = jnp.concatenate([x8[j:j + 1, c * LANES:(c + 1) * LANES] for c in TILE_CHUNK], axis=0)
            tiles.append(weighted_sum(t0 + j, wb, resid))
            wb = wb_next
        for r, c in enumerate(TILE_CHUNK):
            rows = jnp.concatenate([tile[r:r + 1, :] for tile in tiles], axis=0)
            y_ref[pl.ds(t0, V_TOKENS_PER_STEP), c * LANES:(c + 1) * LANES] = rows
        return wb

    lax.fori_loop(0, PEER_TOK // V_TOKENS_PER_STEP, token_group, spread(0))
    x2 = y_ref[...]
    ms = jnp.mean(x2 * x2, axis=-1, keepdims=True)
    y_ref[...] = x2 * lax.rsqrt(ms + EPS) * fg_ref[...]


def _peer_v(e_sm, wt, x13, fg, tbl):
    t = x13.shape[0]
    return pl.pallas_call(
        _peer_v_kernel,
        grid=(t // PEER_TOK,),
        in_specs=[
            pl.BlockSpec((PEER_TOK, LANES), lambda i: (i, 0), memory_space=pltpu.SMEM),
            pl.BlockSpec((PEER_SEL, PEER_TOK), lambda i: (0, i)),
            pl.BlockSpec((PEER_TOK, ROW_TILE * LANES), lambda i: (i, 0)),
            pl.BlockSpec((1, ROW_TILE * LANES), lambda i: (0, 0)),
            pl.BlockSpec(tbl.shape, lambda i: (0, 0), pipeline_mode=pl.Buffered(1)),
        ],
        out_specs=pl.BlockSpec((PEER_TOK, ROW_TILE * LANES), lambda i: (i, 0)),
        out_shape=jax.ShapeDtypeStruct((t, ROW_TILE * LANES), jnp.float32),
        compiler_params=_cparams(("arbitrary",)),
        name="peer_v",
    )(e_sm, wt, x13, fg, tbl)


PACK_ROWS = 512


def _pack_kernel(tbl_ref, out_ref):
    half = ROW_WORDS * LANES
    for p in range(ROW_WORDS):
        lo = tbl_ref[:, p * LANES:(p + 1) * LANES]
        hi = tbl_ref[:, half + p * LANES:half + (p + 1) * LANES]
        words = pltpu.pack_elementwise([lo, hi], packed_dtype=jnp.bfloat16)
        out_ref[pl.ds(p, PACK_ROWS, stride=ROW_WORDS), :] = pltpu.bitcast(words, jnp.int32)


def _pack_table(tbl):
    n, d = tbl.shape
    assert d == ROW_TILE * LANES and n % PACK_ROWS == 0
    return pl.pallas_call(
        _pack_kernel,
        grid=(n // PACK_ROWS,),
        in_specs=[pl.BlockSpec((PACK_ROWS, d), lambda i: (i, 0))],
        out_specs=pl.BlockSpec((PACK_ROWS * ROW_WORDS, LANES), lambda i: (i, 0)),
        out_shape=jax.ShapeDtypeStruct((n * ROW_WORDS, LANES), jnp.int32),
        compiler_params=_cparams(("parallel",)),
        name="pack_table",
    )(tbl)


def _rope_tables(seq):
    pos = jnp.arange(seq)
    row = (pos // GRID_W).astype(jnp.float32)
    col = (pos % GRID_W).astype(jnp.float32)
    n_pairs = ROPE_AXIS_DIM // 2
    inv_freq = ROPE_THETA ** (-jnp.arange(n_pairs, dtype=jnp.float32) / n_pairs)
    ang_r = row[:, None] * inv_freq[None, :]
    ang_c = col[:, None] * inv_freq[None, :]
    cos = jnp.concatenate([jnp.cos(ang_r), jnp.cos(ang_r), jnp.cos(ang_c), jnp.cos(ang_c)], axis=1)
    sin = jnp.concatenate([-jnp.sin(ang_r), jnp.sin(ang_r), -jnp.sin(ang_c), jnp.sin(ang_c)], axis=1)
    reps = LANES // HEAD_DIM
    return jnp.tile(cos, (1, reps)), jnp.tile(sin, (1, reps))


def kernel(x, norm1_g, w_in, q_norm_g, k_norm_g, conv_dw, conv_b, conv_ln_g, conv_ln_b, w_out,
           norm2_g, peer_wq, peer_keys, peer_u, peer_v, final_g):
    b, s, d = x.shape
    t = b * s
    assert d == SUBLANES * LANES and s % GRID_W == 0 and N_KV_HEADS == 2
    tm = min(512, s)
    tq = min(512, s)
    tk = min(1024, s)
    tr = min(256, s)
    assert s % tm == 0 and t % PEER_TOK == 0

    f32 = jnp.float32
    bf16 = jnp.bfloat16
    x2 = x.reshape(t, d)
    cos_t, sin_t = _rope_tables(s)
    head_id = jnp.arange(ATTN_WIDTH) // HEAD_DIM
    ones_bd = (head_id[:, None] == head_id[None, :]).astype(bf16)

    qt, k, vt, hg = _in_proj(
        x2, norm1_g.reshape(1, d), w_in.astype(bf16),
        jnp.tile(q_norm_g, N_HEADS).reshape(1, ATTN_WIDTH),
        jnp.tile(k_norm_g, N_KV_HEADS).reshape(1, KV_WIDTH),
        cos_t, sin_t, ones_bd, b, s, tm)

    attn = _attention(qt, k, vt, b, s, tq, tk)

    cw = jnp.concatenate([conv_dw.reshape(CONV_W, CONV_CH), jnp.zeros((1, CONV_CH), f32)], axis=0)
    x1, hn, hnt = _mix_out(attn, hg, cw, conv_b.reshape(1, CONV_CH), conv_ln_g.reshape(1, CONV_CH),
                      conv_ln_b.reshape(1, CONV_CH), w_out.astype(bf16), x2,
                      norm2_g.reshape(1, d), s, tm)

    e_sm, gates = _route(hn, peer_wq.T.astype(bf16), peer_keys.astype(bf16), tr)

    w = _peer_u(e_sm, hnt, gates, _pack_table(peer_u))
    y = _peer_v(e_sm, w, x1, final_g.reshape(1, d), _pack_table(peer_v))
    return y.reshape(b, s, d)
```

```python
import functools

import jax
import jax.numpy as jnp
import numpy as np
from jax import lax
from jax.experimental import pallas as pl
from jax.experimental.pallas import tpu as pltpu

HEAD_DIM = 64
N_HEADS = 8
N_KV_HEADS = 2
GROUP = N_HEADS // N_KV_HEADS
ATTN_WIDTH = N_HEADS * HEAD_DIM
KV_WIDTH = N_KV_HEADS * HEAD_DIM
CONV_CH = 512
CONV_W = 31
CONV_PAD = (CONV_W - 1) // 2
GRID_W = 64
ROPE_THETA = 10000.0
ROPE_AXIS_DIM = HEAD_DIM // 2
N_KEYS = 128
PEER_HEADS = 8
PEER_DQ = 256
PEER_TOPK = 16
PEER_SEL = PEER_HEADS * PEER_TOPK
EPS = 1e-6

LANES = 128
SUBLANES = 8
HALO = 16
BF16_ROWS = 16
V_EXT = HEAD_DIM + BF16_ROWS
LOG2_E = 1.4426950408889634
VMEM_LIMIT = 56 * 1024 * 1024

ROW_TILE = 8
ROW_WORDS = ROW_TILE // 2
TILE_CHUNK = tuple((r % 2) * ROW_WORDS + r // 2 for r in range(ROW_TILE))

_NEG_INF = float("-inf")


def _cparams(sem, vmem=VMEM_LIMIT):
    return pltpu.CompilerParams(dimension_semantics=sem, vmem_limit_bytes=vmem)


def _group_mean_sq(v, ones_blockdiag):
    sq = v * v
    hi = sq.astype(jnp.bfloat16)
    lo = (sq - hi.astype(jnp.float32)).astype(jnp.bfloat16)
    tot = (jnp.dot(hi, ones_blockdiag, preferred_element_type=jnp.float32)
           + jnp.dot(lo, ones_blockdiag, preferred_element_type=jnp.float32))
    return tot * (1.0 / HEAD_DIM)


def _rope(v, cos, sin_signed, lane_in_pair_lo):
    width = v.shape[-1]
    up = pltpu.roll(v, width - ROPE_AXIS_DIM // 2, axis=1)
    down = pltpu.roll(v, ROPE_AXIS_DIM // 2, axis=1)
    partner = jnp.where(lane_in_pair_lo, up, down)
    return v * cos + partner * sin_signed


def _in_proj_kernel(x_ref, g1_ref, w_ref, qg_ref, kg_ref, cos_ref, sin_ref, ones_ref,
                    qt_ref, k_ref, vt_ref, hg_ref):
    x = x_ref[...]
    ms = jnp.mean(x * x, axis=-1, keepdims=True)
    h = (x * lax.rsqrt(ms + EPS) * g1_ref[...]).astype(jnp.bfloat16)
    p = jnp.dot(h, w_ref[...], preferred_element_type=jnp.float32)
    o1 = ATTN_WIDTH
    o2 = o1 + KV_WIDTH
    o3 = o2 + KV_WIDTH
    o4 = o3 + CONV_CH
    q = p[:, :o1]
    k = p[:, o1:o2]
    v = p[:, o2:o3]
    a = p[:, o3:o4]
    gate = p[:, o4:]

    ones_bd = ones_ref[...]
    qn = q * lax.rsqrt(_group_mean_sq(q, ones_bd) + EPS) * qg_ref[...]
    kn = k * lax.rsqrt(_group_mean_sq(k, ones_bd[:KV_WIDTH, :KV_WIDTH]) + EPS) * kg_ref[...]

    cos2 = cos_ref[...]
    sin2 = sin_ref[...]
    cos_q = jnp.concatenate([cos2] * (ATTN_WIDTH // LANES), axis=1)
    sin_q = jnp.concatenate([sin2] * (ATTN_WIDTH // LANES), axis=1)
    lane_q = lax.broadcasted_iota(jnp.int32, q.shape, 1)
    lane_k = lax.broadcasted_iota(jnp.int32, k.shape, 1)
    half = ROPE_AXIS_DIM // 2
    qr = _rope(qn, cos_q, sin_q, (lane_q % ROPE_AXIS_DIM) < half) * (HEAD_DIM ** -0.5 * LOG2_E)
    kr = _rope(kn, cos2, sin2, (lane_k % ROPE_AXIS_DIM) < half)

    qt = qr.T.astype(qt_ref.dtype)
    zeros = jnp.zeros((HEAD_DIM, qt.shape[1]), qt_ref.dtype)
    for hh in range(N_HEADS):
        g = hh // GROUP
        piece = qt[hh * HEAD_DIM:(hh + 1) * HEAD_DIM, :]
        qt_ref[0, hh, g * HEAD_DIM:(g + 1) * HEAD_DIM, :] = piece
        qt_ref[0, hh, (1 - g) * HEAD_DIM:(2 - g) * HEAD_DIM, :] = zeros
    k_ref[...] = kr.astype(k_ref.dtype)
    vt = v.T.astype(vt_ref.dtype)
    pad_row = lax.broadcasted_iota(jnp.int32, (V_EXT - HEAD_DIM, vt.shape[1]), 0)
    ones_pad = jnp.where(pad_row == 0, 1.0, 0.0).astype(vt_ref.dtype)
    for g in range(N_KV_HEADS):
        vt_ref[0, g, 0:HEAD_DIM, :] = vt[g * HEAD_DIM:(g + 1) * HEAD_DIM, :]
        vt_ref[0, g, HEAD_DIM:V_EXT, :] = ones_pad
    hg_ref[...] = a * (1.0 / (1.0 + jnp.exp(-gate)))


def _in_proj(x2, g1, w_in, qg, kg, cos_t, sin_t, ones_bd, batch, seq, tm):
    t, d = x2.shape
    n_s = seq // tm
    in_width = w_in.shape[1]
    return pl.pallas_call(
        _in_proj_kernel,
        grid=(t // tm,),
        in_specs=[
            pl.BlockSpec((tm, d), lambda i: (i, 0)),
            pl.BlockSpec((1, d), lambda i: (0, 0)),
            pl.BlockSpec((d, in_width), lambda i: (0, 0)),
            pl.BlockSpec((1, ATTN_WIDTH), lambda i: (0, 0)),
            pl.BlockSpec((1, KV_WIDTH), lambda i: (0, 0)),
            pl.BlockSpec((tm, LANES), lambda i: (i % n_s, 0)),
            pl.BlockSpec((tm, LANES), lambda i: (i % n_s, 0)),
            pl.BlockSpec((ATTN_WIDTH, ATTN_WIDTH), lambda i: (0, 0)),
        ],
        out_specs=[
            pl.BlockSpec((1, N_HEADS, KV_WIDTH, tm), lambda i: (i // n_s, 0, 0, i % n_s)),
            pl.BlockSpec((tm, KV_WIDTH), lambda i: (i, 0)),
            pl.BlockSpec((1, N_KV_HEADS, V_EXT, tm), lambda i: (i // n_s, 0, 0, i % n_s)),
            pl.BlockSpec((tm, CONV_CH), lambda i: (i, 0)),
        ],
        out_shape=[
            jax.ShapeDtypeStruct((batch, N_HEADS, KV_WIDTH, seq), jnp.bfloat16),
            jax.ShapeDtypeStruct((t, KV_WIDTH), jnp.bfloat16),
            jax.ShapeDtypeStruct((batch, N_KV_HEADS, V_EXT, seq), jnp.bfloat16),
            jax.ShapeDtypeStruct((t, CONV_CH), jnp.float32),
        ],
        compiler_params=_cparams(("parallel",)),
        name="in_proj",
    )(x2, g1, w_in, qg, kg, cos_t, sin_t, ones_bd)


SCORES_AHEAD = 2


def _attn_kernel(qt_ref, k_ref, vt_ref, o_ref, m_ref, acc_ref):
    ki = pl.program_id(2)

    @pl.when(ki == 0)
    def _():
        m_ref[...] = jnp.full(m_ref.shape, _NEG_INF, jnp.float32)
        acc_ref[...] = jnp.zeros(acc_ref.shape, jnp.float32)

    k = k_ref[...]

    def scores(h):
        return jnp.dot(k, qt_ref[0, h], preferred_element_type=jnp.float32)

    ahead = [scores(h) for h in range(SCORES_AHEAD)]
    for h in range(N_HEADS):
        g = h // GROUP
        s = ahead.pop(0)
        if h + SCORES_AHEAD < N_HEADS:
            ahead.append(scores(h + SCORES_AHEAD))
        m_prev = m_ref[h:h + 1, :]
        m_new = jnp.maximum(m_prev, jnp.max(s, axis=0, keepdims=True))
        alpha = jnp.exp2(m_prev - m_new)
        p = jnp.exp2(s - m_new).astype(jnp.bfloat16)
        pv = jnp.dot(vt_ref[0, g], p, preferred_element_type=jnp.float32)
        acc_ref[h] = alpha * acc_ref[h] + pv
        m_ref[h:h + 1, :] = m_new

    @pl.when(ki == pl.num_programs(2) - 1)
    def _():
        outs = []
        for h in range(N_HEADS):
            acc = acc_ref[h]
            outs.append(acc[:HEAD_DIM, :] / acc[HEAD_DIM:HEAD_DIM + 1, :])
        o_ref[...] = jnp.concatenate(outs, axis=0).T.astype(o_ref.dtype)


def _attention(qt, k, vt, batch, seq, tq, tk):
    t = k.shape[0]
    nq = seq // tq
    nk = seq // tk
    return pl.pallas_call(
        _attn_kernel,
        grid=(batch, nq, nk),
        in_specs=[
            pl.BlockSpec((1, N_HEADS, KV_WIDTH, tq), lambda b, qi, ki: (b, 0, 0, qi)),
            pl.BlockSpec((tk, KV_WIDTH), lambda b, qi, ki: (b * nk + ki, 0)),
            pl.BlockSpec((1, N_KV_HEADS, V_EXT, tk), lambda b, qi, ki: (b, 0, 0, ki)),
        ],
        out_specs=pl.BlockSpec((tq, ATTN_WIDTH), lambda b, qi, ki: (b * nq + qi, 0)),
        out_shape=jax.ShapeDtypeStruct((t, ATTN_WIDTH), jnp.bfloat16),
        scratch_shapes=[
            pltpu.VMEM((N_HEADS, tq), jnp.float32),
            pltpu.VMEM((N_HEADS, V_EXT, tq), jnp.float32),
        ],
        compiler_params=_cparams(("parallel", "parallel", "arbitrary")),
        name="attention",
    )(qt, k, vt)


CONV_ROWS = 32


def _mix_out_kernel(attn_ref, hg_ref, prev_ref, next_ref, cw_ref, cb_ref, lg_ref, lb_ref,
                    wo_ref, x_ref, g2_ref, x1_ref, hn_ref, hnt_ref, ext_ref, conv_ref, *, n_s):
    tm = hg_ref.shape[0]
    si = pl.program_id(0) % n_s
    keep_prev = (si > 0).astype(jnp.float32)
    keep_next = (si < n_s - 1).astype(jnp.float32)
    ext_ref[0:HALO, :] = prev_ref[...] * keep_prev
    ext_ref[HALO:HALO + tm, :] = hg_ref[...]
    ext_ref[HALO + tm:HALO + tm + HALO, :] = next_ref[...] * keep_next

    cw = cw_ref[...]
    cb = cb_ref[...]
    lg = lg_ref[...]
    lb = lb_ref[...]

    def conv_chunk(c):
        start = pl.multiple_of(c * CONV_ROWS, CONV_ROWS)
        rows = CONV_ROWS + 2 * HALO
        parts = []
        for lt in range(CONV_CH // LANES):
            cols = slice(lt * LANES, (lt + 1) * LANES)
            win = ext_ref[pl.ds(start, rows), cols]
            acc_l = jnp.zeros((CONV_ROWS, LANES), jnp.float32)
            for shift in range(SUBLANES):
                shifted = win if shift == 0 else pltpu.roll(win, rows - shift, axis=0)
                for j in range(CONV_W):
                    off = HALO - CONV_PAD + j
                    if off % SUBLANES == shift:
                        base = off - shift
                        acc_l = acc_l + shifted[base:base + CONV_ROWS, :] * cw[j:j + 1, cols]
            parts.append(acc_l)
        acc = jnp.concatenate(parts, axis=1)
        acc = acc + cb
        mu = jnp.mean(acc, axis=-1, keepdims=True)
        cen = acc - mu
        var = jnp.mean(cen * cen, axis=-1, keepdims=True)
        y = cen * lax.rsqrt(var + EPS) * lg + lb
        y = y * (1.0 / (1.0 + jnp.exp(-y)))
        conv_ref[pl.ds(start, CONV_ROWS), :] = y.astype(conv_ref.dtype)

    def conv_step(c, carry):
        conv_chunk(2 * c)
        conv_chunk(2 * c + 1)
        return carry

    lax.fori_loop(0, tm // (2 * CONV_ROWS), conv_step, 0)

    mixed = (jnp.dot(attn_ref[...], wo_ref[0:ATTN_WIDTH, :], preferred_element_type=jnp.float32)
             + jnp.dot(conv_ref[...], wo_ref[ATTN_WIDTH:, :], preferred_element_type=jnp.float32))
    x1 = x_ref[...] + mixed
    x1_ref[...] = x1
    ms = jnp.mean(x1 * x1, axis=-1, keepdims=True)
    hn = x1 * lax.rsqrt(ms + EPS) * g2_ref[...]
    hn_ref[...] = hn
    for r, c in enumerate(TILE_CHUNK):
        hnt_ref[:, r, :] = hn[:, c * LANES:(c + 1) * LANES]


def _mix_out(attn, hg, cw, cb, lg, lb, w_out, x2, g2, seq, tm):
    t, d = x2.shape
    n_s = seq // tm
    hb = tm // HALO
    last_halo = t // HALO - 1
    return pl.pallas_call(
        functools.partial(_mix_out_kernel, n_s=n_s),
        grid=(t // tm,),
        in_specs=[
            pl.BlockSpec((tm, ATTN_WIDTH), lambda i: (i, 0)),
            pl.BlockSpec((tm, CONV_CH), lambda i: (i, 0)),
            pl.BlockSpec((HALO, CONV_CH), lambda i: (jnp.maximum(i * hb - 1, 0), 0)),
            pl.BlockSpec((HALO, CONV_CH), lambda i: (jnp.minimum((i + 1) * hb, last_halo), 0)),
            pl.BlockSpec((CONV_W + 1, CONV_CH), lambda i: (0, 0)),
            pl.BlockSpec((1, CONV_CH), lambda i: (0, 0)),
            pl.BlockSpec((1, CONV_CH), lambda i: (0, 0)),
            pl.BlockSpec((1, CONV_CH), lambda i: (0, 0)),
            pl.BlockSpec((d, d), lambda i: (0, 0)),
            pl.BlockSpec((tm, d), lambda i: (i, 0)),
            pl.BlockSpec((1, d), lambda i: (0, 0)),
        ],
        out_specs=[
            pl.BlockSpec((tm, d), lambda i: (i, 0)),
            pl.BlockSpec((tm, d), lambda i: (i, 0)),
            pl.BlockSpec((tm, ROW_TILE, LANES), lambda i: (i, 0, 0)),
        ],
        out_shape=[
            jax.ShapeDtypeStruct((t, d), jnp.float32),
            jax.ShapeDtypeStruct((t, d), jnp.float32),
            jax.ShapeDtypeStruct((t, ROW_TILE, LANES), jnp.float32),
        ],
        scratch_shapes=[
            pltpu.VMEM((tm + 2 * HALO, CONV_CH), jnp.float32),
            pltpu.VMEM((tm, CONV_CH), jnp.bfloat16),
        ],
        compiler_params=_cparams(("parallel",)),
        name="mix_out",
    )(attn, hg, hg, hg, cw, cb, lg, lb, w_out, x2, g2)


def _top_rows(s, order, payload, count):
    big = jnp.float32(2 ** 30)
    vals = []
    picks = []
    for _ in range(count):
        m = jnp.max(s, axis=0, keepdims=True)
        first = jnp.min(jnp.where(s == m, order, big), axis=0, keepdims=True)
        chosen = order == first
        if payload is order:
            pick = first
        else:
            pick = jnp.max(jnp.where(chosen, payload, -1.0), axis=0, keepdims=True)
        vals.append(m)
        picks.append(pick)
        s = jnp.where(chosen, _NEG_INF, s)
    return jnp.concatenate(vals, axis=0), jnp.concatenate(picks, axis=0)


def _candidate_blocks():
    blocks = [(0, 0, 16)]
    for a in range(1, 8):
        blocks.append((a, 0, 8))
    return blocks


def _route_kernel(hn_ref, wqt_ref, keys_ref, e_ref, g_ref, words_ref):
    hn = hn_ref[...].astype(jnp.bfloat16)
    qpt = lax.dot_general(wqt_ref[...], hn, (((1,), (1,)), ((), ())),
                          preferred_element_type=jnp.float32)
    qpt = qpt.astype(jnp.bfloat16)
    tm = hn.shape[0]
    half = PEER_DQ // 2
    words_ref[PEER_SEL // 2:, :] = jnp.zeros((LANES - PEER_SEL // 2, tm), jnp.int32)
    k0 = keys_ref[0]
    k1 = keys_ref[1]
    row_iota = lax.broadcasted_iota(jnp.int32, (N_KEYS, LANES), 0).astype(jnp.float32)

    blocks = _candidate_blocks()
    flat_parts = []
    for a, b0, nb in blocks:
        flat_parts.append(a * PEER_TOPK + b0 + lax.broadcasted_iota(jnp.int32, (nb, LANES), 0))
    tail_flat = (8 + lax.broadcasted_iota(jnp.int32, (8, LANES), 0)) * PEER_TOPK
    flat = jnp.concatenate(flat_parts + [tail_flat], axis=0).astype(jnp.float32)

    for h in range(PEER_HEADS):
        q1 = qpt[h * PEER_DQ:h * PEER_DQ + half, :]
        q2 = qpt[h * PEER_DQ + half:(h + 1) * PEER_DQ, :]
        s1 = jnp.dot(k0, q1, preferred_element_type=jnp.float32)
        s2 = jnp.dot(k1, q2, preferred_element_type=jnp.float32)
        for c in range(tm // LANES):
            sl = slice(c * LANES, (c + 1) * LANES)
            v1, i1 = _top_rows(s1[:, sl], row_iota, row_iota, PEER_TOPK)
            v2, i2 = _top_rows(s2[:, sl], row_iota, row_iota, PEER_TOPK)
            cs = []
            ce = []
            for a, b0, nb in blocks:
                cs.append(v1[a:a + 1, :] + v2[b0:b0 + nb, :])
                ce.append(i1[a:a + 1, :] * N_KEYS + i2[b0:b0 + nb, :])
            cs.append(v1[8:16, :] + v2[0:1, :])
            ce.append(i1[8:16, :] * N_KEYS + i2[0:1, :])
            cand_s = jnp.concatenate(cs, axis=0)
            cand_e = jnp.concatenate(ce, axis=0)
            top_s, top_e = _top_rows(cand_s, flat, cand_e, PEER_TOPK)
            ex = jnp.exp(top_s - top_s[0:1, :])
            gates = ex / jnp.sum(ex, axis=0, keepdims=True)
            off = top_e.astype(jnp.int32) * ROW_WORDS
            half_k = PEER_TOPK // 2
            words_ref[h * half_k:(h + 1) * half_k, sl] = (off[half_k:, :] << 16) | off[:half_k, :]
            g_ref[h, :, sl] = gates
    e_ref[...] = words_ref[...].T


def _route(hn, wqt, keys, tm):
    t, d = hn.shape
    return pl.pallas_call(
        _route_kernel,
        grid=(t // tm,),
        in_specs=[
            pl.BlockSpec((tm, d), lambda i: (i, 0)),
            pl.BlockSpec(wqt.shape, lambda i: (0, 0)),
            pl.BlockSpec(keys.shape, lambda i: (0, 0, 0)),
        ],
        out_specs=[
            pl.BlockSpec((tm, LANES), lambda i: (i, 0)),
            pl.BlockSpec((PEER_HEADS, PEER_TOPK, tm), lambda i: (0, 0, i)),
        ],
        out_shape=[
            jax.ShapeDtypeStruct((t, LANES), jnp.int32),
            jax.ShapeDtypeStruct((PEER_HEADS, PEER_TOPK, t), jnp.float32),
        ],
        scratch_shapes=[pltpu.VMEM((LANES, tm), jnp.int32)],
        compiler_params=_cparams(("parallel",)),
        name="route",
    )(hn, wqt, keys)


PEER_TOK = 128


def _expert_row(tbl_ref, off):
    words = tbl_ref[pl.ds(pl.multiple_of(off, ROW_WORDS), ROW_WORDS), :]
    return pltpu.bitcast(words, jnp.bfloat16).astype(jnp.float32)


def _expert_offsets(idx_row, head):
    lo, hi = [], []
    for j in range(PEER_TOPK // 2):
        word = idx_row[head * (PEER_TOPK // 2) + j]
        lo.append(word & 0xFFFF)
        hi.append(lax.shift_right_logical(word, 16))
    return lo + hi


def _gelu_tanh(x):
    c = np.float32(np.sqrt(2.0 / np.pi))
    return 0.5 * x * (1.0 + jnp.tanh(c * (x + 0.044715 * (x * x * x))))


def _fold_rows(rows, x, x_swapped, masks):
    lo4, mod4_lo2, even = masks
    a, e, c, g, b, f, d, h = rows

    def level1(p, q):
        t1 = jnp.where(lo4, p, q)
        t2 = pltpu.roll(jnp.where(lo4, q, p), 4, axis=0)
        return t1 * x + t2 * x_swapped

    def level2(u, v):
        uu = u + pltpu.roll(u, 6, axis=0)
        vv = v + pltpu.roll(v, 2, axis=0)
        return jnp.where(mod4_lo2, uu, vv)

    def level3(u, v):
        uu = u + pltpu.roll(u, 7, axis=0)
        vv = v + pltpu.roll(v, 1, axis=0)
        return jnp.where(even, uu, vv)

    return level3(level2(level1(a, b), level1(c, d)), level2(level1(e, f), level1(g, h)))


U_TOKENS_PER_STEP = 16


def _peer_u_kernel(e_ref, x_ref, gate_ref, tbl_ref, w_ref, s_ref, part_a, part_b):
    lane = lax.broadcasted_iota(jnp.int32, (PEER_SEL, PEER_TOK), 1)
    sub = lax.broadcasted_iota(jnp.int32, (SUBLANES, LANES), 0)
    masks = (sub < 4, (sub % 4) < 2, (sub % 2) == 0)

    def fold(t, part_ref):
        e_row = e_ref.at[t]
        x = x_ref[t]
        x_swapped = pltpu.roll(x, 4, axis=0)
        for head in range(PEER_HEADS):
            offs = _expert_offsets(e_row, head)
            for half in range(PEER_TOPK // SUBLANES):
                grp = head * (PEER_TOPK // SUBLANES) + half
                rows = [_expert_row(tbl_ref, offs[half * SUBLANES + j]) for j in range(SUBLANES)]
                part_ref[grp * SUBLANES:(grp + 1) * SUBLANES, :] = _fold_rows(rows, x, x_swapped, masks)

    def finish(t, part_ref):
        s = jnp.sum(part_ref[...], axis=1, keepdims=True)
        s_ref[...] = jnp.where(lane == t, s, s_ref[...])

    s_ref[...] = jnp.zeros(s_ref.shape, jnp.float32)
    part_b[...] = jnp.zeros(part_b.shape, jnp.float32)

    def token_block(i, carry):
        t0 = U_TOKENS_PER_STEP * i
        for j in range(U_TOKENS_PER_STEP):
            cur, prev = (part_a, part_b) if j % 2 == 0 else (part_b, part_a)
            fold(t0 + j, cur)
            finish(t0 + j - 1, prev)
        return carry

    lax.fori_loop(0, PEER_TOK // U_TOKENS_PER_STEP, token_block, 0)
    finish(PEER_TOK - 1, part_b)
    gates = gate_ref[...].reshape(PEER_SEL, PEER_TOK)
    w_ref[...] = gates * _gelu_tanh(s_ref[...])


def _peer_u(e_sm, x3, gates, tbl):
    t = x3.shape[0]
    return pl.pallas_call(
        _peer_u_kernel,
        grid=(t // PEER_TOK,),
        in_specs=[
            pl.BlockSpec((PEER_TOK, LANES), lambda i: (i, 0), memory_space=pltpu.SMEM),
            pl.BlockSpec((PEER_TOK, SUBLANES, LANES), lambda i: (i, 0, 0)),
            pl.BlockSpec((PEER_HEADS, PEER_TOPK, PEER_TOK), lambda i: (0, 0, i)),
            pl.BlockSpec(tbl.shape, lambda i: (0, 0), pipeline_mode=pl.Buffered(1)),
        ],
        out_specs=pl.BlockSpec((PEER_SEL, PEER_TOK), lambda i: (0, i)),
        out_shape=jax.ShapeDtypeStruct((PEER_SEL, t), jnp.float32),
        scratch_shapes=[
            pltpu.VMEM((PEER_SEL, PEER_TOK), jnp.float32),
            pltpu.VMEM((PEER_SEL, LANES), jnp.float32),
            pltpu.VMEM((PEER_SEL, LANES), jnp.float32),
        ],
        compiler_params=_cparams(("arbitrary",)),
        name="peer_u",
    )(e_sm, x3, gates, tbl)


V_TOKENS_PER_STEP = SUBLANES


def _peer_v_kernel(e_ref, wt_ref, x1_ref, fg_ref, tbl_ref, y_ref):
    lane = lax.broadcasted_iota(jnp.int32, (PEER_SEL, PEER_TOK), 1)

    def spread(t):
        col = jnp.sum(jnp.where(lane == t, wt_ref[...], 0.0), axis=1, keepdims=True)
        return jnp.broadcast_to(col, (PEER_SEL, LANES))

    def weighted_sum(t, wb, acc):
        e_row = e_ref.at[t]
        for head in range(PEER_HEADS):
            offs = _expert_offsets(e_row, head)
            for half in range(PEER_TOPK // SUBLANES):
                terms = []
                for j in range(SUBLANES):
                    slot = half * SUBLANES + j
                    k = head * PEER_TOPK + slot
                    terms.append(wb[k:k + 1, :] * _expert_row(tbl_ref, offs[slot]))
                while len(terms) > 1:
                    terms = [terms[i] + terms[i + 1] for i in range(0, len(terms), 2)]
                acc = acc + terms[0]
        return acc

    def token_group(i, wb):
        t0 = pl.multiple_of(i * V_TOKENS_PER_STEP, V_TOKENS_PER_STEP)
        x8 = x1_ref[pl.ds(t0, V_TOKENS_PER_STEP), :]
        tiles = []
        for j in range(V_TOKENS_PER_STEP):
            wb_next = spread(jnp.minimum(t0 + j + 1, PEER_TOK - 1))
            resid = jnp.concatenate([x8[j:j + 1, c * LANES:(c + 1) * LANES] for c in TILE_CHUNK], axis=0)
            tiles.append(weighted_sum(t0 + j, wb, resid))
            wb = wb_next
        for r, c in enumerate(TILE_CHUNK):
            rows = jnp.concatenate([tile[r:r + 1, :] for tile in tiles], axis=0)
            y_ref[pl.ds(t0, V_TOKENS_PER_STEP), c * LANES:(c + 1) * LANES] = rows
        return wb

    lax.fori_loop(0, PEER_TOK // V_TOKENS_PER_STEP, token_group, spread(0))
    x2 = y_ref[...]
    ms = jnp.mean(x2 * x2, axis=-1, keepdims=True)
    y_ref[...] = x2 * lax.rsqrt(ms + EPS) * fg_ref[...]


def _peer_v(e_sm, wt, x13, fg, tbl):
    t = x13.shape[0]
    return pl.pallas_call(
        _peer_v_kernel,
        grid=(t // PEER_TOK,),
        in_specs=[
            pl.BlockSpec((PEER_TOK, LANES), lambda i: (i, 0), memory_space=pltpu.SMEM),
            pl.BlockSpec((PEER_SEL, PEER_TOK), lambda i: (0, i)),
            pl.BlockSpec((PEER_TOK, ROW_TILE * LANES), lambda i: (i, 0)),
            pl.BlockSpec((1, ROW_TILE * LANES), lambda i: (0, 0)),
            pl.BlockSpec(tbl.shape, lambda i: (0, 0), pipeline_mode=pl.Buffered(1)),
        ],
        out_specs=pl.BlockSpec((PEER_TOK, ROW_TILE * LANES), lambda i: (i, 0)),
        out_shape=jax.ShapeDtypeStruct((t, ROW_TILE * LANES), jnp.float32),
        compiler_params=_cparams(("arbitrary",)),
        name="peer_v",
    )(e_sm, wt, x13, fg, tbl)


PACK_ROWS = 512


def _pack_kernel(u_ref, v_ref, uo_ref, vo_ref):
    half = ROW_WORDS * LANES
    for tbl_ref, out_ref in ((u_ref, uo_ref), (v_ref, vo_ref)):
        for p in range(ROW_WORDS):
            lo = tbl_ref[:, p * LANES:(p + 1) * LANES]
            hi = tbl_ref[:, half + p * LANES:half + (p + 1) * LANES]
            words = pltpu.pack_elementwise([lo, hi], packed_dtype=jnp.bfloat16)
            out_ref[pl.ds(p, PACK_ROWS, stride=ROW_WORDS), :] = pltpu.bitcast(words, jnp.int32)


def _pack_tables(tbl_u, tbl_v):
    n, d = tbl_u.shape
    assert tbl_v.shape == (n, d) and d == ROW_TILE * LANES and n % PACK_ROWS == 0
    in_spec = pl.BlockSpec((PACK_ROWS, d), lambda i: (i, 0))
    out_spec = pl.BlockSpec((PACK_ROWS * ROW_WORDS, LANES), lambda i: (i, 0))
    out_sds = jax.ShapeDtypeStruct((n * ROW_WORDS, LANES), jnp.int32)
    return pl.pallas_call(
        _pack_kernel,
        grid=(n // PACK_ROWS,),
        in_specs=[in_spec, in_spec],
        out_specs=[out_spec, out_spec],
        out_shape=[out_sds, out_sds],
        compiler_params=_cparams(("parallel",)),
        name="pack_tables",
    )(tbl_u, tbl_v)


def _rope_tables(seq):
    pos = jnp.arange(seq)
    row = (pos // GRID_W).astype(jnp.float32)
    col = (pos % GRID_W).astype(jnp.float32)
    n_pairs = ROPE_AXIS_DIM // 2
    inv_freq = ROPE_THETA ** (-jnp.arange(n_pairs, dtype=jnp.float32) / n_pairs)
    ang_r = row[:, None] * inv_freq[None, :]
    ang_c = col[:, None] * inv_freq[None, :]
    cos = jnp.concatenate([jnp.cos(ang_r), jnp.cos(ang_r), jnp.cos(ang_c), jnp.cos(ang_c)], axis=1)
    sin = jnp.concatenate([-jnp.sin(ang_r), jnp.sin(ang_r), -jnp.sin(ang_c), jnp.sin(ang_c)], axis=1)
    reps = LANES // HEAD_DIM
    return jnp.tile(cos, (1, reps)), jnp.tile(sin, (1, reps))


def kernel(x, norm1_g, w_in, q_norm_g, k_norm_g, conv_dw, conv_b, conv_ln_g, conv_ln_b, w_out,
           norm2_g, peer_wq, peer_keys, peer_u, peer_v, final_g):
    b, s, d = x.shape
    t = b * s
    assert d == SUBLANES * LANES and s % GRID_W == 0 and N_KV_HEADS == 2
    tm = min(512, s)
    tq = min(512, s)
    tk = min(1024, s)
    tr = min(256, s)
    assert s % tm == 0 and t % PEER_TOK == 0

    f32 = jnp.float32
    bf16 = jnp.bfloat16
    x2 = x.reshape(t, d)
    cos_t, sin_t = _rope_tables(s)
    head_id = jnp.arange(ATTN_WIDTH) // HEAD_DIM
    ones_bd = (head_id[:, None] == head_id[None, :]).astype(bf16)

    qt, k, vt, hg = _in_proj(
        x2, norm1_g.reshape(1, d), w_in.astype(bf16),
        jnp.tile(q_norm_g, N_HEADS).reshape(1, ATTN_WIDTH),
        jnp.tile(k_norm_g, N_KV_HEADS).reshape(1, KV_WIDTH),
        cos_t, sin_t, ones_bd, b, s, tm)

    attn = _attention(qt, k, vt, b, s, tq, tk)

    cw = jnp.concatenate([conv_dw.reshape(CONV_W, CONV_CH), jnp.zeros((1, CONV_CH), f32)], axis=0)
    x1, hn, hnt = _mix_out(attn, hg, cw, conv_b.reshape(1, CONV_CH), conv_ln_g.reshape(1, CONV_CH),
                      conv_ln_b.reshape(1, CONV_CH), w_out.astype(bf16), x2,
                      norm2_g.reshape(1, d), s, tm)

    e_sm, gates = _route(hn, peer_wq.T.astype(bf16), peer_keys.astype(bf16), tr)

    u_words, v_words = _pack_tables(peer_u, peer_v)
    w = _peer_u(e_sm, hnt, gates, u_words)
    y = _peer_v(e_sm, w, x1, final_g.reshape(1, d), v_words)
    return y.reshape(b, s, d)
```
